```python
import math
import jax
import jax.numpy as jnp
from jax import lax
import numpy as np

D_MODEL = 1024
BATCH = 1
SEQ = 16384
DEPTH = 2
DEC_BATCH = 32
DEC_SEQ = 4
PAST_LEN = 16384
PAGE_SIZE = 128

W_A = D_MODEL
W_B = D_MODEL
HEAD_DIM = 64
H_SB = W_B // HEAD_DIM
CONV_W = 3
Q_BLOCK = 128
SB_BIAS_INIT = -7.0
W_C = 2 * D_MODEL
G_C = 16
CHUNK = 128
P_DIM = 256
N_SB = (DEPTH + 1) // 2
N_GM = DEPTH // 2
ALPHA = (2 * DEPTH) ** 0.25
BETA = (8 * DEPTH) ** -0.25
SB_SCALE = 1.0 / math.sqrt(HEAD_DIM)
LN_EPS = 1e-5

kernel_name = 'hybrid_conv_stickbreak_gmlp_step'


def _split(a, sizes):
    out, o = [], 0
    for s in sizes:
        out.append(a[..., o:o + s])
        o += s
    return out


def layer_norm(x, g, b):
    xf = x.astype(jnp.float32)
    mu = jnp.mean(xf, axis=-1, keepdims=True)
    var = jnp.var(xf, axis=-1, keepdims=True)
    return ((xf - mu) * lax.rsqrt(var + LN_EPS) * g + b).astype(x.dtype)


def short_conv(u, prefix, w):
    t = u.shape[1]
    up = jnp.concatenate([prefix.astype(u.dtype), u], axis=1)
    y = sum(w[j] * up[:, j:j + t] for j in range(CONV_W))
    return y, up[:, t:]


def stick_breaking(q, k, v, bias, q_pos, k_pos):
    z = jnp.einsum('bthd,bshd->bhts', q, k, preferred_element_type=jnp.float32) * SB_SCALE
    z = z + bias.astype(jnp.float32)[None, :, None, None]
    mask = k_pos[None, :] < q_pos[:, None]
    log_beta = jax.nn.log_sigmoid(z)
    log_rem = jnp.where(mask, log_beta - z, 0.0)
    after = lax.cumsum(log_rem, axis=3, reverse=True) - log_rem
    a = jnp.where(mask, jnp.exp(log_beta + after), 0.0)
    return jnp.einsum('bhts,bshd->bthd', a.astype(v.dtype), v)


def sb_prompt(q, k, v, bias):
    b, t, h, d = q.shape
    nb = t // Q_BLOCK
    qb = jnp.moveaxis(q.reshape(b, nb, Q_BLOCK, h, d), 1, 0)
    starts = jnp.arange(nb, dtype=jnp.int32) * Q_BLOCK
    k_pos = jnp.arange(t, dtype=jnp.int32)

    def one_block(args):
        qq, st = args
        return stick_breaking(qq, k, v, bias, st + jnp.arange(Q_BLOCK, dtype=jnp.int32), k_pos)

    o = lax.map(one_block, (qb, starts))
    return jnp.moveaxis(o, 0, 1).reshape(b, t, h, d)


def sb_sample(q, k, v, bias, k_past, v_past):
    past, n = k_past.shape[1], q.shape[1]
    kk = jnp.concatenate([k_past.astype(k.dtype), k], axis=1)
    vv = jnp.concatenate([v_past.astype(v.dtype), v], axis=1)
    q_pos = past + jnp.arange(n, dtype=jnp.int32)
    k_pos = jnp.arange(past + n, dtype=jnp.int32)
    return stick_breaking(q, kk, vv, bias, q_pos, k_pos)


def even_mix(x, w_in, conv_w, w_out, sb_bias, conv_prefix, attend):
    proj = x @ w_in
    h_a, b_a, c_a, z_a, q, k, v, z_b = _split(proj, [W_A] * 4 + [W_B] * 4)
    conv_out, tail = short_conv(c_a * h_a, conv_prefix, conv_w)
    y_a = b_a * conv_out * jax.nn.silu(z_a)
    bsz, t, _ = x.shape
    q, k, v = (a.reshape(bsz, t, H_SB, HEAD_DIM) for a in (q, k, v))
    o = attend(q, k, v, sb_bias).reshape(bsz, t, W_B)
    y_b = o * jax.nn.silu(z_b)
    out = jnp.concatenate([y_a, y_b], axis=-1) @ w_out
    return out, k, v, tail


def gmlp_prompt(v, w_s, b_s):
    b, t, c = v.shape
    vr = v.reshape(b, t // CHUNK, CHUNK, G_C, c // G_C)
    m = jnp.einsum('gts,bnsgc->bntgc', w_s, vr) + b_s.T[:, :, None]
    return m.reshape(b, t, c)


def gmlp_sample(v, w_s, b_s):
    b, n, c = v.shape
    vr = v.reshape(b, n, G_C, c // G_C)
    m = jnp.einsum('gts,bsgc->btgc', w_s[:, :n, :n], vr) + b_s[:, :n].T[:, :, None]
    return m.reshape(b, n, c)


def odd_mix(x, w_in, vn_g, vn_b, w_s, b_s, w_out, chunk_mix):
    u, v, z = _split(x @ w_in, [W_C] * 3)
    u = jax.nn.gelu(u)
    v = layer_norm(jax.nn.gelu(v), vn_g, vn_b)
    m = chunk_mix(v, jnp.tril(w_s), b_s)
    out = (u * m * jax.nn.silu(z)) @ w_out
    return out, v


def post_block(x, s, p, g, b, w_ple, w_gate):
    h = layer_norm(ALPHA * x + s, g, b)
    return h + jax.nn.sigmoid(h @ w_gate) * (p @ w_ple)


def setup_inputs(seed: int = 0) -> dict:
    key = jax.random.key(seed)
    ks = jax.random.split(key, 24)
    n_pages = PAST_LEN // PAGE_SIZE
    n_used = DEC_BATCH * n_pages
    n_pool = n_used + (n_used + 3) // 4
    nrm = jax.random.normal
    f32 = jnp.float32
    page_table = jax.random.permutation(ks[0], n_pool)[:n_used].reshape(DEC_BATCH, n_pages).astype(jnp.int32)
    return {
        'x_prompt': nrm(ks[1], (BATCH, SEQ, D_MODEL), f32),
        'x_sample': nrm(ks[2], (DEC_BATCH, DEC_SEQ, D_MODEL), f32),
        'cache_k': nrm(ks[3], (N_SB, n_pool, PAGE_SIZE, H_SB, HEAD_DIM), f32),
        'cache_v': nrm(ks[4], (N_SB, n_pool, PAGE_SIZE, H_SB, HEAD_DIM), f32),
        'state_conv': nrm(ks[5], (N_SB, DEC_BATCH, CONV_W - 1, W_A), f32),
        'page_table': page_table,
        'p_prompt': nrm(ks[6], (DEPTH, BATCH, SEQ, P_DIM), f32),
        'p_sample': nrm(ks[7], (DEPTH, DEC_BATCH, DEC_SEQ, P_DIM), f32),
        'w_in_e': nrm(ks[8], (N_SB, D_MODEL, 4 * W_A + 4 * W_B), f32) * D_MODEL ** -0.5,
        'conv_w': nrm(ks[9], (N_SB, CONV_W, W_A), f32) * CONV_W ** -0.5,
        'w_out_e': nrm(ks[10], (N_SB, W_A + W_B, D_MODEL), f32) * ((W_A + W_B) ** -0.5 * BETA),
        'sb_bias': SB_BIAS_INIT + 0.5 * nrm(ks[21], (N_SB, H_SB), f32),
        'w_in_o': nrm(ks[11], (N_GM, D_MODEL, 3 * W_C), f32) * D_MODEL ** -0.5,
        'vnorm_g': 1.0 + 0.02 * nrm(ks[12], (N_GM, W_C), f32),
        'vnorm_b': 0.02 * nrm(ks[13], (N_GM, W_C), f32),
        'w_s': nrm(ks[14], (N_GM, G_C, CHUNK, CHUNK), f32) * CHUNK ** -0.5,
        'b_s': 1.0 + 0.1 * nrm(ks[15], (N_GM, G_C, CHUNK), f32),
        'w_out_o': nrm(ks[16], (N_GM, W_C, D_MODEL), f32) * (W_C ** -0.5 * BETA),
        'ln_g': 1.0 + 0.02 * nrm(ks[17], (DEPTH, D_MODEL), f32),
        'ln_b': 0.02 * nrm(ks[18], (DEPTH, D_MODEL), f32),
        'w_ple': nrm(ks[19], (DEPTH, P_DIM, D_MODEL), f32) * P_DIM ** -0.5,
        'w_ple_gate': nrm(ks[20], (DEPTH, D_MODEL, D_MODEL), f32) * D_MODEL ** -0.5,
    }


def reference(x_prompt, x_sample, cache_k, cache_v, state_conv, page_table, p_prompt, p_sample,
              w_in_e, conv_w, w_out_e, sb_bias, w_in_o, vnorm_g, vnorm_b, w_s, b_s, w_out_o,
              ln_g, ln_b, w_ple, w_ple_gate):
    yp, ys = x_prompt, x_sample
    dec_b, n_pages = page_table.shape
    kp_l, vp_l, ks_l, vs_l, cp_l, cs_l, gv_l = [], [], [], [], [], [], []
    for i in range(DEPTH):
        li = i // 2
        if i % 2 == 0:
            prefix0 = jnp.zeros((yp.shape[0], CONV_W - 1, W_A), yp.dtype)
            sp, kp, vp, tail_p = even_mix(yp, w_in_e[li], conv_w[li], w_out_e[li], sb_bias[li],
                                          prefix0, sb_prompt)
            k_past = cache_k[li][page_table].reshape(dec_b, n_pages * PAGE_SIZE, H_SB, HEAD_DIM)
            v_past = cache_v[li][page_table].reshape(dec_b, n_pages * PAGE_SIZE, H_SB, HEAD_DIM)
            attend_s = lambda q, k, v, bias, kpa=k_past, vpa=v_past: sb_sample(q, k, v, bias, kpa, vpa)
            ss, ksn, vsn, tail_s = even_mix(ys, w_in_e[li], conv_w[li], w_out_e[li], sb_bias[li],
                                            state_conv[li], attend_s)
            kp_l.append(kp); vp_l.append(vp); ks_l.append(ksn); vs_l.append(vsn)
            cp_l.append(tail_p); cs_l.append(tail_s)
        else:
            sp, _ = odd_mix(yp, w_in_o[li], vnorm_g[li], vnorm_b[li], w_s[li], b_s[li], w_out_o[li], gmlp_prompt)
            ss, gv = odd_mix(ys, w_in_o[li], vnorm_g[li], vnorm_b[li], w_s[li], b_s[li], w_out_o[li], gmlp_sample)
            gv_l.append(gv)
        yp = post_block(yp, sp, p_prompt[i], ln_g[i], ln_b[i], w_ple[i], w_ple_gate[i])
        ys = post_block(ys, ss, p_sample[i], ln_g[i], ln_b[i], w_ple[i], w_ple_gate[i])
    return (yp, ys, jnp.stack(kp_l), jnp.stack(vp_l), jnp.stack(ks_l), jnp.stack(vs_l),
            jnp.stack(cp_l), jnp.stack(cs_l), jnp.stack(gv_l))
```

```python
import functools
import math

import jax
import jax.numpy as jnp
from jax import lax
from jax.experimental import pallas as pl
from jax.experimental.pallas import tpu as pltpu

F32 = jnp.float32
BF16 = jnp.bfloat16

D_MODEL = 1024
HEAD_DIM = 64
N_HEADS = 16
CONV_W = 3
W_C = 2 * D_MODEL
G_C = 16
CHUNK = 128
P_DIM = 256
PAGE = 128
DEPTH = 2
ALPHA = (2 * DEPTH) ** 0.25
SB_SCALE = 1.0 / math.sqrt(HEAD_DIM)
LN_EPS = 1e-5
LOG2E = 1.4426950408889634

SB_BLK = 256
SB_SEG = SB_BLK // 8
HEADS_PER_STEP = 2
PAIR_W = HEADS_PER_STEP * HEAD_DIM

VMEM_LIMIT = 56 * 1024 * 1024


def _cparams(sem):
    return pltpu.CompilerParams(dimension_semantics=sem, vmem_limit_bytes=VMEM_LIMIT)


def _silu(x):
    return x * jax.nn.sigmoid(x)


def _layer_norm(x, g, b):
    mu = jnp.mean(x, axis=-1, keepdims=True)
    xc = x - mu
    var = jnp.mean(xc * xc, axis=-1, keepdims=True)
    return xc * lax.rsqrt(var + LN_EPS) * g + b


def _post_block(x, s, p_bf, g, b, wg_ref, wp_ref):
    h = _layer_norm(ALPHA * x + s, g, b)
    gate = jax.nn.sigmoid(jnp.dot(h.astype(BF16), wg_ref[...], preferred_element_type=F32))
    return h + gate * jnp.dot(p_bf, wp_ref[...], preferred_element_type=F32)


def _even_in_kernel(*refs, tm, cw, seq_len):
    x_ref = refs[0]
    wh, wb, wc, wza, wq, wk, wv, wzb = refs[1:9]
    cw_ref = refs[9]
    pos = 10
    if seq_len is not None:
        p1_ref, p2_ref = refs[pos], refs[pos + 1]
        pos += 2
    ya_ref, q_ref, kb_ref, vb_ref, zb_ref, k32_ref, v32_ref, u_ref = refs[pos:pos + 8]
    pos += 8
    carry_ref = refs[pos] if seq_len is None else None

    xb = x_ref[...].astype(BF16)

    def proj(w_ref):
        return jnp.dot(xb, w_ref[...], preferred_element_type=F32)

    u = proj(wc) * proj(wh)
    row = lax.broadcasted_iota(jnp.int32, (tm, cw), 0)
    r1 = pltpu.roll(u, 1, 0)
    r2 = pltpu.roll(u, 2, 0)
    if seq_len is None:
        @pl.when(pl.program_id(1) == 0)
        def _():
            carry_ref[...] = jnp.zeros_like(carry_ref)

        c2 = carry_ref[6:7, :]
        c1 = carry_ref[7:8, :]
        prev1 = jnp.where(row >= 1, r1, c1)
        prev2 = jnp.where(row >= 2, r2, jnp.where(row == 0, c2, c1))
        carry_ref[...] = u[tm - 8:, :]
        u_ref[...] = u[tm - 8:, :]
    else:
        t_in = row % seq_len
        prev1 = jnp.where(t_in >= 1, r1, p1_ref[...])
        prev2 = jnp.where(t_in >= 2, r2, p2_ref[...])
        u_ref[...] = u
    conv = cw_ref[0:1, :] * prev2 + cw_ref[1:2, :] * prev1 + cw_ref[2:3, :] * u
    ya_ref[...] = (proj(wb) * conv * _silu(proj(wza))).astype(BF16)

    q_ref[...] = (proj(wq) * (-SB_SCALE * LOG2E)).astype(BF16)
    k = proj(wk)
    k32_ref[...] = k
    kb_ref[...] = k.astype(BF16)
    v = proj(wv)
    v32_ref[...] = v
    vb_ref[...] = v.astype(BF16)
    zb_ref[...] = _silu(proj(wzb)).astype(BF16)


def _even_in(x, w_bf, conv_w, prefix=None, *, tm, seq_len=None, cw=256):
    m = x.shape[0]
    ncb = D_MODEL // cw
    grid = (ncb, m // tm)
    in_specs = [pl.BlockSpec((tm, D_MODEL), lambda j, i: (i, 0))]
    for c in range(8):
        in_specs.append(pl.BlockSpec((D_MODEL, cw), lambda j, i, c=c: (0, c * ncb + j)))
    in_specs.append(pl.BlockSpec((CONV_W, cw), lambda j, i: (0, j)))
    args = [x] + [w_bf] * 8 + [conv_w]
    if seq_len is not None:
        in_specs += [pl.BlockSpec((tm, cw), lambda j, i: (i, j))] * 2
        args += list(prefix)
    tile = pl.BlockSpec((tm, cw), lambda j, i: (i, j))
    u_rows = 8 if seq_len is None else m
    u_spec = (pl.BlockSpec((8, cw), lambda j, i: (0, j)) if seq_len is None else tile)
    out_shape = ([jax.ShapeDtypeStruct((m, D_MODEL), BF16)] * 5
                 + [jax.ShapeDtypeStruct((m, D_MODEL), F32)] * 2
                 + [jax.ShapeDtypeStruct((u_rows, D_MODEL), F32)])
    out_specs = [tile] * 7 + [u_spec]
    scratch = [pltpu.VMEM((8, cw), F32)] if seq_len is None else []
    return pl.pallas_call(
        functools.partial(_even_in_kernel, tm=tm, cw=cw, seq_len=seq_len),
        grid=grid, in_specs=in_specs, out_specs=out_specs, out_shape=out_shape,
        scratch_shapes=scratch,
        compiler_params=_cparams(("arbitrary", "arbitrary")),
        name="even_in_prompt" if seq_len is None else "even_in_sample",
    )(*args)


def _sb_prompt_kernel(nbias_ref, qT_ref, k_ref, vT_ref, zb_ref, o_ref, tmp_ref, acc_ref):
    qi = pl.program_id(1)
    w2 = HEADS_PER_STEP * SB_BLK
    qT = qT_ref[0]
    hrow = lax.broadcasted_iota(jnp.int32, (PAIR_W, SB_BLK), 0)
    zero = jnp.zeros_like(qT)
    qT2 = jnp.concatenate([jnp.where(hrow < HEAD_DIM, qT, zero),
                           jnp.where(hrow >= HEAD_DIM, qT, zero)], axis=1)
    nbias = nbias_ref[0]
    sub = lax.broadcasted_iota(jnp.int32, (8, w2), 0)
    lane = lax.broadcasted_iota(jnp.int32, (8, w2), 1)
    diag = (lane % SB_BLK) - sub * SB_SEG
    acc_ref[...] = jnp.zeros_like(acc_ref)

    def block(kj, carry, masked):
        kb = k_ref[pl.ds(pl.multiple_of(kj * SB_BLK, SB_BLK), SB_BLK), :]
        zT = jnp.dot(kb, qT2, preferred_element_type=F32)
        run = jnp.ones((8, w2), F32)
        for r in reversed(range(SB_SEG)):
            w = jnp.exp2(zT[r * 8:(r + 1) * 8, :] + nbias)
            beta = 1.0 / (1.0 + w)
            om = 1.0 - beta
            if masked:
                keep = diag > r
                beta = jnp.where(keep, beta, 0.0)
                om = jnp.where(keep, om, 1.0)
            tmp_ref[r] = beta * run
            run = run * om
        x0 = jnp.where(sub < 7, pltpu.roll(run, 7, 0), 1.0)
        s1 = x0 * jnp.where(sub < 7, pltpu.roll(x0, 7, 0), 1.0)
        s2 = s1 * jnp.where(sub < 6, pltpu.roll(s1, 6, 0), 1.0)
        off = s2 * jnp.where(sub < 4, pltpu.roll(s2, 4, 0), 1.0)
        base = off * carry
        total = off[0:1, :] * run[0:1, :]
        a = (tmp_ref[...] * base[None]).reshape(SB_BLK, w2).astype(BF16)
        acc_ref[...] += jnp.dot(vT_ref[0, kj], a, preferred_element_type=F32)
        return carry * total

    carry = block(qi, jnp.ones((8, w2), F32), True)
    lax.fori_loop(0, qi, lambda it, c: block(qi - 1 - it, c, False), carry)

    oT = jnp.concatenate([acc_ref[:HEAD_DIM, :SB_BLK], acc_ref[HEAD_DIM:, SB_BLK:]], axis=0)
    o_ref[...] = (oT.T * zb_ref[...].astype(F32)).astype(BF16)


def _sb_prompt(q_s, k_bf, v_bf, gate_b, sb_bias):
    t = q_s.shape[0]
    nb = t // SB_BLK
    npair = N_HEADS // HEADS_PER_STEP
    qT = q_s.reshape(t, npair, PAIR_W).transpose(1, 2, 0)
    k_perm = k_bf.reshape(nb, 8, SB_SEG, D_MODEL).transpose(0, 2, 1, 3).reshape(t, D_MODEL)
    vT = v_bf.reshape(nb, 8, SB_SEG, npair, PAIR_W).transpose(3, 0, 4, 2, 1).reshape(npair, nb, PAIR_W, SB_BLK)
    nbias = jnp.repeat((-LOG2E) * sb_bias.astype(F32), SB_BLK).reshape(npair, 1, HEADS_PER_STEP * SB_BLK)
    nbias = jnp.broadcast_to(nbias, (npair, 8, HEADS_PER_STEP * SB_BLK))
    w2 = HEADS_PER_STEP * SB_BLK
    return pl.pallas_call(
        _sb_prompt_kernel,
        grid=(npair, nb),
        in_specs=[
            pl.BlockSpec((1, 8, w2), lambda p, i: (p, 0, 0)),
            pl.BlockSpec((1, PAIR_W, SB_BLK), lambda p, i: (p, 0, i)),
            pl.BlockSpec((t, PAIR_W), lambda p, i: (0, p)),
            pl.BlockSpec((1, nb, PAIR_W, SB_BLK), lambda p, i: (p, 0, 0, 0)),
            pl.BlockSpec((SB_BLK, PAIR_W), lambda p, i: (i, p)),
        ],
        out_specs=pl.BlockSpec((SB_BLK, PAIR_W), lambda p, i: (i, p)),
        out_shape=jax.ShapeDtypeStruct((t, D_MODEL), BF16),
        scratch_shapes=[pltpu.VMEM((SB_SEG, 8, w2), F32), pltpu.VMEM((PAIR_W, w2), F32)],
        compiler_params=_cparams(("arbitrary", "arbitrary")),
        name="sb_prompt",
    )(nbias, qT, k_perm, vT, gate_b)


SAMPLE_PAGES_PER_STEP = 4


def _sb_sample_kernel(pt_ref, qbd_ref, bias_ref, knew_ref, vnew_ref, zb_ref, *refs, n_tok):
    pg = SAMPLE_PAGES_PER_STEP
    k_refs = refs[:pg]
    v_refs = refs[pg:2 * pg]
    o_ref = refs[2 * pg]
    acc_ref, carry_ref = refs[2 * pg + 1:]
    j = pl.program_id(1)
    qbd = qbd_ref[0]
    bias = bias_ref[...]
    ri = lax.broadcasted_iota(jnp.int32, (PAGE, PAGE), 0)
    ci = lax.broadcasted_iota(jnp.int32, (PAGE, PAGE), 1)
    upper = (ci > ri).astype(BF16)

    def unit(k_bf, v_bf, keep):
        z = jnp.dot(k_bf, qbd, preferred_element_type=F32) + bias[0:1, :]
        lse = jnp.log2(1.0 + jnp.exp2(-jnp.abs(z)))
        lb = jnp.minimum(z, 0.0) - lse
        lr = lb - z
        if keep is not None:
            lr = jnp.where(keep, lr, 0.0)
        hi = lr.astype(BF16)
        lo = (lr - hi.astype(F32)).astype(BF16)
        after = (jnp.dot(upper, hi, preferred_element_type=F32)
                 + jnp.dot(upper, lo, preferred_element_type=F32) + carry_ref[0:1, :])
        a = jnp.exp2(lb + after)
        if keep is not None:
            a = jnp.where(keep, a, 0.0)
        carry_ref[...] = carry_ref[...] + jnp.sum(lr, axis=0, keepdims=True)
        acc_ref[...] += jnp.dot(a.T.astype(BF16), v_bf, preferred_element_type=F32)

    @pl.when(j == 0)
    def _():
        acc_ref[...] = jnp.zeros_like(acc_ref)
        carry_ref[...] = jnp.zeros_like(carry_ref)
        keep = jnp.logical_and(ri < ci // N_HEADS, ri < n_tok)
        unit(knew_ref[0], vnew_ref[0], keep)

    for i in reversed(range(pg)):
        unit(k_refs[i][0].astype(BF16), v_refs[i][0].astype(BF16), None)

    @pl.when(j == pl.num_programs(1) - 1)
    def _():
        hcol = lax.broadcasted_iota(jnp.int32, (N_HEADS, D_MODEL), 1) // HEAD_DIM
        hrow = lax.broadcasted_iota(jnp.int32, (N_HEADS, D_MODEL), 0)
        sel = hcol == hrow
        rows = []
        for t in range(n_tok):
            blk = acc_ref[t * N_HEADS:(t + 1) * N_HEADS, :]
            rows.append(jnp.sum(jnp.where(sel, blk, 0.0), axis=0, keepdims=True))
        o = jnp.concatenate(rows, axis=0)
        o_ref[0] = (o * zb_ref[0].astype(F32)).astype(BF16)


def _sb_sample(q_s, k_new, v_new, gate_b, sb_bias, cache_k, cache_v, page_table, n_tok):
    dec_b, n_pages = page_table.shape
    n_pool = cache_k.shape[0]
    pg = SAMPLE_PAGES_PER_STEP
    npg = n_pages // pg
    ck = cache_k.reshape(n_pool, PAGE, D_MODEL)
    cv = cache_v.reshape(n_pool, PAGE, D_MODEL)
    q4 = (-q_s.astype(F32)).reshape(dec_b, n_tok, N_HEADS, HEAD_DIM)
    eye = jnp.eye(N_HEADS, dtype=F32)
    qbd = jnp.einsum('bthd,hg->bhdtg', q4, eye).reshape(dec_b, D_MODEL, n_tok * N_HEADS)
    qbd = jnp.pad(qbd, ((0, 0), (0, 0), (0, PAGE - n_tok * N_HEADS))).astype(BF16)
    bias = jnp.tile(LOG2E * sb_bias.astype(F32), n_tok)
    bias = jnp.broadcast_to(jnp.pad(bias, (0, PAGE - n_tok * N_HEADS))[None, :], (8, PAGE))
    pad_rows = ((0, 0), (0, PAGE - n_tok), (0, 0))
    knew = jnp.pad(k_new.reshape(dec_b, n_tok, D_MODEL), pad_rows).astype(BF16)
    vnew = jnp.pad(v_new.reshape(dec_b, n_tok, D_MODEL), pad_rows).astype(BF16)
    gate3 = gate_b.reshape(dec_b, n_tok, D_MODEL)

    def page_spec(i):
        return pl.BlockSpec((1, PAGE, D_MODEL), lambda b, j, pt, i=i: (pt[b, pg * (npg - 1 - j) + i], 0, 0))

    per_seq = lambda shape: pl.BlockSpec(shape, lambda b, j, pt: (b, 0, 0))
    grid_spec = pltpu.PrefetchScalarGridSpec(
        num_scalar_prefetch=1,
        grid=(dec_b, npg),
        in_specs=[per_seq((1, D_MODEL, PAGE)),
                  pl.BlockSpec((8, PAGE), lambda b, j, pt: (0, 0)),
                  per_seq((1, PAGE, D_MODEL)), per_seq((1, PAGE, D_MODEL)),
                  per_seq((1, n_tok, D_MODEL))]
                 + [page_spec(i) for i in range(pg)] * 2,
        out_specs=per_seq((1, n_tok, D_MODEL)),
        scratch_shapes=[pltpu.VMEM((PAGE, D_MODEL), F32), pltpu.VMEM((8, PAGE), F32)],
    )
    out = pl.pallas_call(
        functools.partial(_sb_sample_kernel, n_tok=n_tok),
        grid_spec=grid_spec,
        out_shape=jax.ShapeDtypeStruct((dec_b, n_tok, D_MODEL), BF16),
        compiler_params=_cparams(("arbitrary", "arbitrary")),
        name="sb_sample",
    )(page_table, qbd, bias, knew, vnew, gate3, *([ck] * pg), *([cv] * pg))
    return out.reshape(dec_b * n_tok, D_MODEL)


def _out_post_kernel(x_ref, ya_ref, yb_ref, p_ref, wo_ref, g_ref, b_ref, wg_ref, wp_ref, o_ref):
    s = (jnp.dot(ya_ref[...], wo_ref[:D_MODEL, :], preferred_element_type=F32)
         + jnp.dot(yb_ref[...], wo_ref[D_MODEL:, :], preferred_element_type=F32))
    o_ref[...] = _post_block(x_ref[...], s, p_ref[...].astype(BF16), g_ref[...], b_ref[...], wg_ref, wp_ref)


def _const_spec(shape):
    nd = len(shape)
    return pl.BlockSpec(shape, lambda i: (0,) * nd)


def _out_post(x, ya, yb, p, wo_bf, g, b, wg_bf, wp_bf, *, tm):
    m = x.shape[0]
    row = lambda w: pl.BlockSpec((tm, w), lambda i: (i, 0))
    return pl.pallas_call(
        _out_post_kernel,
        grid=(m // tm,),
        in_specs=[row(D_MODEL), row(D_MODEL), row(D_MODEL), row(P_DIM),
                  _const_spec(wo_bf.shape), _const_spec((1, D_MODEL)), _const_spec((1, D_MODEL)),
                  _const_spec(wg_bf.shape), _const_spec(wp_bf.shape)],
        out_specs=row(D_MODEL),
        out_shape=jax.ShapeDtypeStruct((m, D_MODEL), F32),
        compiler_params=_cparams(("arbitrary",)),
        name="out_post",
    )(x, ya, yb, p, wo_bf, g.reshape(1, -1), b.reshape(1, -1), wg_bf, wp_bf)


def _odd_kernel(x_ref, p_ref, wu_ref, wv_ref, wz_ref, vg_ref, vb_ref, mix_ref, bmap_ref, wo_ref,
                g_ref, b_ref, wg_ref, wp_ref, *out_refs, tm, emit_v):
    o_ref = out_refs[0]
    x = x_ref[...]
    xb = x.astype(BF16)
    v = jax.nn.gelu(jnp.dot(xb, wv_ref[...], preferred_element_type=F32), approximate=True)
    vn = _layer_norm(v, vg_ref[...], vb_ref[...])
    if emit_v:
        out_refs[1][...] = vn
    vnb = vn.astype(BF16)
    chunks = []
    for c in range(tm // CHUNK):
        cols = []
        for g in range(G_C):
            blk = vnb[c * CHUNK:(c + 1) * CHUNK, g * CHUNK:(g + 1) * CHUNK]
            cols.append(jnp.dot(mix_ref[g], blk, preferred_element_type=F32))
        chunks.append(jnp.concatenate(cols, axis=1) + bmap_ref[...])
    mixed = chunks[0] if len(chunks) == 1 else jnp.concatenate(chunks, axis=0)
    u = jax.nn.gelu(jnp.dot(xb, wu_ref[...], preferred_element_type=F32), approximate=True)
    zz = _silu(jnp.dot(xb, wz_ref[...], preferred_element_type=F32))
    t = (u * mixed * zz).astype(BF16)
    s = jnp.dot(t, wo_ref[...], preferred_element_type=F32)
    o_ref[...] = _post_block(x, s, p_ref[...].astype(BF16), g_ref[...], b_ref[...], wg_ref, wp_ref)


def _odd_layer(x, p, w_in_bf, vg, vb, mix_bf, bmap, wo_bf, g, b, wg_bf, wp_bf, *, tm, emit_v):
    m = x.shape[0]
    row = lambda w: pl.BlockSpec((tm, w), lambda i: (i, 0))
    col = lambda c: pl.BlockSpec((D_MODEL, W_C), lambda i, c=c: (0, c))
    out_shape = [jax.ShapeDtypeStruct((m, D_MODEL), F32)]
    out_specs = [row(D_MODEL)]
    if emit_v:
        out_shape.append(jax.ShapeDtypeStruct((m, W_C), F32))
        out_specs.append(row(W_C))
    return pl.pallas_call(
        functools.partial(_odd_kernel, tm=tm, emit_v=emit_v),
        grid=(m // tm,),
        in_specs=[row(D_MODEL), row(P_DIM), col(0), col(1), col(2),
                  _const_spec((1, W_C)), _const_spec((1, W_C)),
                  _const_spec(mix_bf.shape), _const_spec(bmap.shape), _const_spec(wo_bf.shape),
                  _const_spec((1, D_MODEL)), _const_spec((1, D_MODEL)),
                  _const_spec(wg_bf.shape), _const_spec(wp_bf.shape)],
        out_specs=out_specs,
        out_shape=out_shape,
        compiler_params=_cparams(("arbitrary",)),
        name="odd_layer_sample" if emit_v else "odd_layer_prompt",
    )(x, p, w_in_bf, w_in_bf, w_in_bf, vg.reshape(1, -1), vb.reshape(1, -1), mix_bf, bmap, wo_bf,
      g.reshape(1, -1), b.reshape(1, -1), wg_bf, wp_bf)


def _pick_tile(m, pref):
    t = min(pref, m)
    while m % t:
        t //= 2
    return t


def kernel(x_prompt, x_sample, cache_k, cache_v, state_conv, page_table, p_prompt, p_sample,
           w_in_e, conv_w, w_out_e, sb_bias, w_in_o, vnorm_g, vnorm_b, w_s, b_s, w_out_o,
           ln_g, ln_b, w_ple, w_ple_gate):
    bsz, t_p, _ = x_prompt.shape
    assert bsz == 1
    dec_b, n_tok, _ = x_sample.shape
    m_s = dec_b * n_tok
    assert m_s == CHUNK and t_p % SB_BLK == 0
    yp = x_prompt.reshape(t_p, D_MODEL)
    ys = x_sample.reshape(m_s, D_MODEL)
    outs = {}

    for i in range(DEPTH):
        li = i // 2
        wg_bf = w_ple_gate[i].astype(BF16)
        wp_bf = w_ple[i].astype(BF16)
        pp = p_prompt[i].reshape(t_p, P_DIM)
        ps = p_sample[i].reshape(m_s, P_DIM)
        if i % 2 == 0:
            w_in_bf = w_in_e[li].astype(BF16)
            wo_bf = w_out_e[li].astype(BF16)
            ya, q_s, k_bf, v_bf, gate_b, k32, v32, u_tail = _even_in(
                yp, w_in_bf, conv_w[li], tm=_pick_tile(t_p, 512))
            yb = _sb_prompt(q_s, k_bf, v_bf, gate_b, sb_bias[li])
            st = state_conv[li]
            zeros = jnp.zeros((dec_b, n_tok - 1, D_MODEL), F32)
            p1 = jnp.concatenate([st[:, 1:2], zeros], axis=1).reshape(m_s, D_MODEL)
            p2 = jnp.concatenate([st, zeros[:, 1:]], axis=1).reshape(m_s, D_MODEL)
            ya_s, q_ss, _, _, gate_s, k32_s, v32_s, u_s = _even_in(
                ys, w_in_bf, conv_w[li], (p1, p2), tm=m_s, seq_len=n_tok)
            yb_s = _sb_sample(q_ss, k32_s, v32_s, gate_s, sb_bias[li], cache_k[li], cache_v[li],
                              page_table, n_tok)
            yp = _out_post(yp, ya, yb, pp, wo_bf, ln_g[i], ln_b[i], wg_bf, wp_bf, tm=_pick_tile(t_p, 512))
            ys = _out_post(ys, ya_s, yb_s, ps, wo_bf, ln_g[i], ln_b[i], wg_bf, wp_bf, tm=m_s)
            outs.setdefault('kp', []).append(k32.reshape(bsz, t_p, N_HEADS, HEAD_DIM))
            outs.setdefault('vp', []).append(v32.reshape(bsz, t_p, N_HEADS, HEAD_DIM))
            outs.setdefault('ks', []).append(k32_s.reshape(dec_b, n_tok, N_HEADS, HEAD_DIM))
            outs.setdefault('vs', []).append(v32_s.reshape(dec_b, n_tok, N_HEADS, HEAD_DIM))
            outs.setdefault('cp', []).append(u_tail[8 - (CONV_W - 1):].reshape(bsz, CONV_W - 1, D_MODEL))
            outs.setdefault('cs', []).append(
                u_s.reshape(dec_b, n_tok, D_MODEL)[:, n_tok - (CONV_W - 1):])
        else:
            w_in_bf = w_in_o[li].astype(BF16)
            wo_bf = w_out_o[li].astype(BF16)
            tril = jnp.tril(w_s[li])
            bmap_p = jnp.repeat(b_s[li].T, W_C // G_C, axis=1)
            small = tril[:, :n_tok, :n_tok]
            mix_s = jnp.einsum('ab,gts->gatbs', jnp.eye(dec_b, dtype=F32), small).reshape(G_C, m_s, m_s)
            bmap_s = jnp.tile(bmap_p[:n_tok], (dec_b, 1))
            (yp,) = _odd_layer(yp, pp, w_in_bf, vnorm_g[li], vnorm_b[li], tril.astype(BF16), bmap_p, wo_bf,
                               ln_g[i], ln_b[i], wg_bf, wp_bf, tm=_pick_tile(t_p, 256), emit_v=False)
            ys, gv = _odd_layer(ys, ps, w_in_bf, vnorm_g[li], vnorm_b[li], mix_s.astype(BF16), bmap_s, wo_bf,
                                ln_g[i], ln_b[i], wg_bf, wp_bf, tm=m_s, emit_v=True)
            outs.setdefault('gv', []).append(gv.reshape(dec_b, n_tok, W_C))

    return (yp.reshape(bsz, t_p, D_MODEL), ys.reshape(dec_b, n_tok, D_MODEL),
            jnp.stack(outs['kp']), jnp.stack(outs['vp']), jnp.stack(outs['ks']), jnp.stack(outs['vs']),
            jnp.stack(outs['cp']), jnp.stack(outs['cs']), jnp.stack(outs['gv']))
```

```python
import functools
import math

import jax
import jax.numpy as jnp
from jax import lax
from jax.experimental import pallas as pl
from jax.experimental.pallas import tpu as pltpu

F32 = jnp.float32
BF16 = jnp.bfloat16

D_MODEL = 1024
HEAD_DIM = 64
N_HEADS = 16
CONV_W = 3
W_C = 2 * D_MODEL
G_C = 16
CHUNK = 128
P_DIM = 256
PAGE = 128
DEPTH = 2
ALPHA = (2 * DEPTH) ** 0.25
SB_SCALE = 1.0 / math.sqrt(HEAD_DIM)
LN_EPS = 1e-5
LOG2E = 1.4426950408889634

SB_BLK = 256
SB_SEG = SB_BLK // 8
HEADS_PER_STEP = 2
PAIR_W = HEADS_PER_STEP * HEAD_DIM

VMEM_LIMIT = 56 * 1024 * 1024


def _cparams(sem):
    return pltpu.CompilerParams(dimension_semantics=sem, vmem_limit_bytes=VMEM_LIMIT)


def _silu(x):
    return x * jax.nn.sigmoid(x)


def _layer_norm(x, g, b):
    mu = jnp.mean(x, axis=-1, keepdims=True)
    xc = x - mu
    var = jnp.mean(xc * xc, axis=-1, keepdims=True)
    return xc * lax.rsqrt(var + LN_EPS) * g + b


def _post_block(x, s, p_bf, g, b, wg_ref, wp_ref):
    h = _layer_norm(ALPHA * x + s, g, b)
    gate = jax.nn.sigmoid(jnp.dot(h.astype(BF16), wg_ref[...], preferred_element_type=F32))
    return h + gate * jnp.dot(p_bf, wp_ref[...], preferred_element_type=F32)


def _even_in_kernel(*refs, tm, cw, seq_len):
    x_ref = refs[0]
    wh, wb, wc, wza, wq, wk, wv, wzb = refs[1:9]
    cw_ref = refs[9]
    pos = 10
    if seq_len is not None:
        p1_ref, p2_ref = refs[pos], refs[pos + 1]
        pos += 2
    ya_ref, q_ref, kb_ref, vb_ref, zb_ref, k32_ref, v32_ref, u_ref = refs[pos:pos + 8]
    pos += 8
    carry_ref = refs[pos] if seq_len is None else None

    xb = x_ref[...].astype(BF16)

    def proj(w_ref):
        return jnp.dot(xb, w_ref[...], preferred_element_type=F32)

    u = proj(wc) * proj(wh)
    row = lax.broadcasted_iota(jnp.int32, (tm, cw), 0)
    r1 = pltpu.roll(u, 1, 0)
    r2 = pltpu.roll(u, 2, 0)
    if seq_len is None:
        @pl.when(pl.program_id(1) == 0)
        def _():
            carry_ref[...] = jnp.zeros_like(carry_ref)

        c2 = carry_ref[6:7, :]
        c1 = carry_ref[7:8, :]
        prev1 = jnp.where(row >= 1, r1, c1)
        prev2 = jnp.where(row >= 2, r2, jnp.where(row == 0, c2, c1))
        carry_ref[...] = u[tm - 8:, :]
        u_ref[...] = u[tm - 8:, :]
    else:
        t_in = row % seq_len
        prev1 = jnp.where(t_in >= 1, r1, p1_ref[...])
        prev2 = jnp.where(t_in >= 2, r2, p2_ref[...])
        u_ref[...] = u
    conv = cw_ref[0:1, :] * prev2 + cw_ref[1:2, :] * prev1 + cw_ref[2:3, :] * u
    ya_ref[...] = (proj(wb) * conv * _silu(proj(wza))).astype(BF16)

    q_ref[...] = (proj(wq) * (-SB_SCALE * LOG2E)).astype(BF16)
    k = proj(wk)
    k32_ref[...] = k
    kb_ref[...] = k.astype(BF16)
    v = proj(wv)
    v32_ref[...] = v
    vb_ref[...] = v.astype(BF16)
    zb_ref[...] = _silu(proj(wzb)).astype(BF16)


def _even_in(x, w_bf, conv_w, prefix=None, *, tm, seq_len=None, cw=256):
    m = x.shape[0]
    ncb = D_MODEL // cw
    grid = (ncb, m // tm)
    in_specs = [pl.BlockSpec((tm, D_MODEL), lambda j, i: (i, 0))]
    for c in range(8):
        in_specs.append(pl.BlockSpec((D_MODEL, cw), lambda j, i, c=c: (0, c * ncb + j)))
    in_specs.append(pl.BlockSpec((CONV_W, cw), lambda j, i: (0, j)))
    args = [x] + [w_bf] * 8 + [conv_w]
    if seq_len is not None:
        in_specs += [pl.BlockSpec((tm, cw), lambda j, i: (i, j))] * 2
        args += list(prefix)
    tile = pl.BlockSpec((tm, cw), lambda j, i: (i, j))
    u_rows = 8 if seq_len is None else m
    u_spec = (pl.BlockSpec((8, cw), lambda j, i: (0, j)) if seq_len is None else tile)
    out_shape = ([jax.ShapeDtypeStruct((m, D_MODEL), BF16)] * 5
                 + [jax.ShapeDtypeStruct((m, D_MODEL), F32)] * 2
                 + [jax.ShapeDtypeStruct((u_rows, D_MODEL), F32)])
    out_specs = [tile] * 7 + [u_spec]
    scratch = [pltpu.VMEM((8, cw), F32)] if seq_len is None else []
    return pl.pallas_call(
        functools.partial(_even_in_kernel, tm=tm, cw=cw, seq_len=seq_len),
        grid=grid, in_specs=in_specs, out_specs=out_specs, out_shape=out_shape,
        scratch_shapes=scratch,
        compiler_params=_cparams(("arbitrary", "arbitrary")),
        name="even_in_prompt" if seq_len is None else "even_in_sample",
    )(*args)


def _sb_prompt_kernel(nbias_ref, qT_ref, k_ref, vT_ref, zb_ref, o_ref,
                      z0_ref, z1_ref, a0_ref, a1_ref, tmp_ref, acc_ref):
    qi = pl.program_id(1)
    w2 = HEADS_PER_STEP * SB_BLK
    qT = qT_ref[0]
    hrow = lax.broadcasted_iota(jnp.int32, (PAIR_W, SB_BLK), 0)
    zero = jnp.zeros_like(qT)
    qT2 = jnp.concatenate([jnp.where(hrow < HEAD_DIM, qT, zero),
                           jnp.where(hrow >= HEAD_DIM, qT, zero)], axis=1)
    nbias = nbias_ref[0]
    sub = lax.broadcasted_iota(jnp.int32, (8, w2), 0)
    lane = lax.broadcasted_iota(jnp.int32, (8, w2), 1)
    diag = (lane % SB_BLK) - sub * SB_SEG
    acc_ref[...] = jnp.zeros_like(acc_ref)

    def scores(kj, z_ref):
        kj = jnp.maximum(kj, 0)
        kb = k_ref[pl.ds(pl.multiple_of(kj * SB_BLK, SB_BLK), SB_BLK), :]
        z_ref[...] = jnp.dot(kb, qT2, preferred_element_type=F32)

    def weights(z_ref, a_ref, carry, masked):
        run = jnp.ones((8, w2), F32)
        for r in reversed(range(SB_SEG)):
            w = jnp.exp2(z_ref[r * 8:(r + 1) * 8, :] + nbias)
            beta = 1.0 / (1.0 + w)
            om = 1.0 - beta
            if masked:
                keep = diag > r
                beta = jnp.where(keep, beta, 0.0)
                om = jnp.where(keep, om, 1.0)
            tmp_ref[r] = beta * run
            run = run * om
        x0 = jnp.where(sub < 7, pltpu.roll(run, 7, 0), 1.0)
        s1 = x0 * jnp.where(sub < 7, pltpu.roll(x0, 7, 0), 1.0)
        s2 = s1 * jnp.where(sub < 6, pltpu.roll(s1, 6, 0), 1.0)
        off = s2 * jnp.where(sub < 4, pltpu.roll(s2, 4, 0), 1.0)
        base = off * carry
        a_ref[...] = (tmp_ref[...] * base[None]).reshape(SB_BLK, w2).astype(BF16)
        return carry * (off[0:1, :] * run[0:1, :])

    def values(kj, a_ref):
        acc_ref[...] += jnp.dot(vT_ref[0, kj], a_ref[...], preferred_element_type=F32)

    scores(qi, z0_ref)
    scores(qi - 1, z1_ref)
    carry = weights(z0_ref, a0_ref, jnp.ones((8, w2), F32), True)

    def pair(i, carry):
        kj = qi - 2 * i - 1
        scores(kj - 1, z0_ref)
        values(kj + 1, a0_ref)
        carry = weights(z1_ref, a1_ref, carry, False)
        scores(kj - 2, z1_ref)
        values(kj, a1_ref)
        return weights(z0_ref, a0_ref, carry, False)

    carry = lax.fori_loop(0, qi // 2, pair, carry)

    @pl.when(qi % 2 == 0)
    def _():
        values(0, a0_ref)

    @pl.when(qi % 2 == 1)
    def _():
        values(1, a0_ref)
        weights(z1_ref, a1_ref, carry, False)
        values(0, a1_ref)

    oT = jnp.concatenate([acc_ref[:HEAD_DIM, :SB_BLK], acc_ref[HEAD_DIM:, SB_BLK:]], axis=0)
    o_ref[...] = (oT.T * zb_ref[...].astype(F32)).astype(BF16)


def _sb_prompt(q_s, k_bf, v_bf, gate_b, sb_bias):
    t = q_s.shape[0]
    nb = t // SB_BLK
    npair = N_HEADS // HEADS_PER_STEP
    qT = q_s.reshape(t, npair, PAIR_W).transpose(1, 2, 0)
    k_perm = k_bf.reshape(nb, 8, SB_SEG, D_MODEL).transpose(0, 2, 1, 3).reshape(t, D_MODEL)
    vT = v_bf.reshape(nb, 8, SB_SEG, npair, PAIR_W).transpose(3, 0, 4, 2, 1).reshape(npair, nb, PAIR_W, SB_BLK)
    nbias = jnp.repeat((-LOG2E) * sb_bias.astype(F32), SB_BLK).reshape(npair, 1, HEADS_PER_STEP * SB_BLK)
    nbias = jnp.broadcast_to(nbias, (npair, 8, HEADS_PER_STEP * SB_BLK))
    w2 = HEADS_PER_STEP * SB_BLK
    return pl.pallas_call(
        _sb_prompt_kernel,
        grid=(npair, nb),
        in_specs=[
            pl.BlockSpec((1, 8, w2), lambda p, i: (p, 0, 0)),
            pl.BlockSpec((1, PAIR_W, SB_BLK), lambda p, i: (p, 0, i)),
            pl.BlockSpec((t, PAIR_W), lambda p, i: (0, p)),
            pl.BlockSpec((1, nb, PAIR_W, SB_BLK), lambda p, i: (p, 0, 0, 0)),
            pl.BlockSpec((SB_BLK, PAIR_W), lambda p, i: (i, p)),
        ],
        out_specs=pl.BlockSpec((SB_BLK, PAIR_W), lambda p, i: (i, p)),
        out_shape=jax.ShapeDtypeStruct((t, D_MODEL), BF16),
        scratch_shapes=[pltpu.VMEM((SB_BLK, w2), F32), pltpu.VMEM((SB_BLK, w2), F32),
                        pltpu.VMEM((SB_BLK, w2), BF16), pltpu.VMEM((SB_BLK, w2), BF16),
                        pltpu.VMEM((SB_SEG, 8, w2), F32), pltpu.VMEM((PAIR_W, w2), F32)],
        compiler_params=_cparams(("arbitrary", "arbitrary")),
        name="sb_prompt",
    )(nbias, qT, k_perm, vT, gate_b)


SAMPLE_PAGES_PER_STEP = 4


def _sb_sample_kernel(pt_ref, qbd_ref, bias_ref, knew_ref, vnew_ref, zb_ref, *refs, n_tok):
    pg = SAMPLE_PAGES_PER_STEP
    k_refs = refs[:pg]
    v_refs = refs[pg:2 * pg]
    o_ref = refs[2 * pg]
    acc_ref, carry_ref = refs[2 * pg + 1:]
    j = pl.program_id(1)
    nrow = n_tok * N_HEADS
    qbd = qbd_ref[0]
    bias = bias_ref[...]
    ri = lax.broadcasted_iota(jnp.int32, (PAGE, PAGE), 0)
    ci = lax.broadcasted_iota(jnp.int32, (PAGE, PAGE), 1)
    later = (ri > ci).astype(BF16)

    def local(kT_bf, keep):
        z = jnp.dot(qbd, kT_bf, preferred_element_type=F32) + bias
        lse = jnp.log2(1.0 + jnp.exp2(-jnp.abs(z)))
        lb = jnp.minimum(z, 0.0) - lse
        lr = lb - z
        if keep is not None:
            lr = jnp.where(keep, lr, 0.0)
        hi = lr.astype(BF16)
        lo = (lr - hi.astype(F32)).astype(BF16)
        after = (jnp.dot(hi, later, preferred_element_type=F32)
                 + jnp.dot(lo, later, preferred_element_type=F32))
        return lb + after, jnp.sum(lr, axis=1, keepdims=True)

    def weighted(pre, carry, vT_bf, keep):
        a = jnp.exp2(pre + carry)
        if keep is not None:
            a = jnp.where(keep, a, 0.0)
        return lax.dot_general(a.astype(BF16), vT_bf, (((1,), (1,)), ((), ())),
                               preferred_element_type=F32)

    @pl.when(j == 0)
    def _():
        rr = lax.broadcasted_iota(jnp.int32, (nrow, PAGE), 0)
        cc = lax.broadcasted_iota(jnp.int32, (nrow, PAGE), 1)
        keep = jnp.logical_and(cc < rr // N_HEADS, cc < n_tok)
        pre, tot = local(knew_ref[0], keep)
        acc_ref[...] = weighted(pre, 0.0, vnew_ref[0], keep)
        carry_ref[...] = jnp.broadcast_to(tot, carry_ref.shape)

    order = list(reversed(range(pg)))
    loc = [local(k_refs[i][0].astype(BF16), None) for i in order]
    carry = carry_ref[...]
    out = None
    for i, (pre, tot) in zip(order, loc):
        d = weighted(pre, carry, v_refs[i][0].astype(BF16), None)
        out = d if out is None else out + d
        carry = carry + tot
    acc_ref[...] += out
    carry_ref[...] = carry

    @pl.when(j == pl.num_programs(1) - 1)
    def _():
        hcol = lax.broadcasted_iota(jnp.int32, (N_HEADS, D_MODEL), 1) // HEAD_DIM
        hrow = lax.broadcasted_iota(jnp.int32, (N_HEADS, D_MODEL), 0)
        sel = hcol == hrow
        rows = []
        for t in range(n_tok):
            blk = acc_ref[t * N_HEADS:(t + 1) * N_HEADS, :]
            rows.append(jnp.sum(jnp.where(sel, blk, 0.0), axis=0, keepdims=True))
        o = jnp.concatenate(rows, axis=0)
        o_ref[0] = (o * zb_ref[0].astype(F32)).astype(BF16)


def _sb_sample(q_s, k_new, v_new, gate_b, sb_bias, cache_k, cache_v, page_table, n_tok):
    dec_b, n_pages = page_table.shape
    n_pool = cache_k.shape[0]
    pg = SAMPLE_PAGES_PER_STEP
    npg = n_pages // pg
    nrow = n_tok * N_HEADS
    ck = cache_k.transpose(0, 2, 3, 1).reshape(n_pool, D_MODEL, PAGE)
    cv = cache_v.transpose(0, 2, 3, 1).reshape(n_pool, D_MODEL, PAGE)
    q4 = (-q_s.astype(F32)).reshape(dec_b, n_tok, N_HEADS, HEAD_DIM)
    eye = jnp.eye(N_HEADS, dtype=F32)
    qbd = jnp.einsum('bthd,hg->bthgd', q4, eye).reshape(dec_b, nrow, D_MODEL).astype(BF16)
    bias = jnp.broadcast_to(jnp.tile(LOG2E * sb_bias.astype(F32), n_tok)[:, None], (nrow, PAGE))
    pad_keys = ((0, 0), (0, 0), (0, PAGE - n_tok))
    knew = jnp.pad(k_new.reshape(dec_b, n_tok, D_MODEL).transpose(0, 2, 1), pad_keys).astype(BF16)
    vnew = jnp.pad(v_new.reshape(dec_b, n_tok, D_MODEL).transpose(0, 2, 1), pad_keys).astype(BF16)
    gate3 = gate_b.reshape(dec_b, n_tok, D_MODEL)

    def page_spec(i):
        return pl.BlockSpec((1, D_MODEL, PAGE), lambda b, j, pt, i=i: (pt[b, pg * (npg - 1 - j) + i], 0, 0))

    per_seq = lambda shape: pl.BlockSpec(shape, lambda b, j, pt: (b, 0, 0))
    grid_spec = pltpu.PrefetchScalarGridSpec(
        num_scalar_prefetch=1,
        grid=(dec_b, npg),
        in_specs=[per_seq((1, nrow, D_MODEL)),
                  pl.BlockSpec((nrow, PAGE), lambda b, j, pt: (0, 0)),
                  per_seq((1, D_MODEL, PAGE)), per_seq((1, D_MODEL, PAGE)),
                  per_seq((1, n_tok, D_MODEL))]
                 + [page_spec(i) for i in range(pg)] * 2,
        out_specs=per_seq((1, n_tok, D_MODEL)),
        scratch_shapes=[pltpu.VMEM((nrow, D_MODEL), F32), pltpu.VMEM((nrow, PAGE), F32)],
    )
    out = pl.pallas_call(
        functools.partial(_sb_sample_kernel, n_tok=n_tok),
        grid_spec=grid_spec,
        out_shape=jax.ShapeDtypeStruct((dec_b, n_tok, D_MODEL), BF16),
        compiler_params=_cparams(("arbitrary", "arbitrary")),
        name="sb_sample",
    )(page_table, qbd, bias, knew, vnew, gate3, *([ck] * pg), *([cv] * pg))
    return out.reshape(dec_b * n_tok, D_MODEL)


def _out_post_kernel(x_ref, ya_ref, yb_ref, p_ref, wo_ref, g_ref, b_ref, wg_ref, wp_ref, o_ref):
    s = (jnp.dot(ya_ref[...], wo_ref[:D_MODEL, :], preferred_element_type=F32)
         + jnp.dot(yb_ref[...], wo_ref[D_MODEL:, :], preferred_element_type=F32))
    o_ref[...] = _post_block(x_ref[...], s, p_ref[...].astype(BF16), g_ref[...], b_ref[...], wg_ref, wp_ref)


def _const_spec(shape):
    nd = len(shape)
    return pl.BlockSpec(shape, lambda i: (0,) * nd)


def _out_post(x, ya, yb, p, wo_bf, g, b, wg_bf, wp_bf, *, tm):
    m = x.shape[0]
    row = lambda w: pl.BlockSpec((tm, w), lambda i: (i, 0))
    return pl.pallas_call(
        _out_post_kernel,
        grid=(m // tm,),
        in_specs=[row(D_MODEL), row(D_MODEL), row(D_MODEL), row(P_DIM),
                  _const_spec(wo_bf.shape), _const_spec((1, D_MODEL)), _const_spec((1, D_MODEL)),
                  _const_spec(wg_bf.shape), _const_spec(wp_bf.shape)],
        out_specs=row(D_MODEL),
        out_shape=jax.ShapeDtypeStruct((m, D_MODEL), F32),
        compiler_params=_cparams(("arbitrary",)),
        name="out_post",
    )(x, ya, yb, p, wo_bf, g.reshape(1, -1), b.reshape(1, -1), wg_bf, wp_bf)


def _odd_kernel(x_ref, p_ref, wu_ref, wv_ref, wz_ref, vg_ref, vb_ref, mix_ref, bmap_ref, wo_ref,
                g_ref, b_ref, wg_ref, wp_ref, *out_refs, tm, emit_v):
    o_ref = out_refs[0]
    x = x_ref[...]
    xb = x.astype(BF16)
    v = jax.nn.gelu(jnp.dot(xb, wv_ref[...], preferred_element_type=F32), approximate=True)
    vn = _layer_norm(v, vg_ref[...], vb_ref[...])
    if emit_v:
        out_refs[1][...] = vn
    vnb = vn.astype(BF16)
    chunks = []
    for c in range(tm // CHUNK):
        cols = []
        for g in range(G_C):
            blk = vnb[c * CHUNK:(c + 1) * CHUNK, g * CHUNK:(g + 1) * CHUNK]
            cols.append(jnp.dot(mix_ref[g], blk, preferred_element_type=F32))
        chunks.append(jnp.concatenate(cols, axis=1) + bmap_ref[...])
    mixed = chunks[0] if len(chunks) == 1 else jnp.concatenate(chunks, axis=0)
    u = jax.nn.gelu(jnp.dot(xb, wu_ref[...], preferred_element_type=F32), approximate=True)
    zz = _silu(jnp.dot(xb, wz_ref[...], preferred_element_type=F32))
    t = (u * mixed * zz).astype(BF16)
    s = jnp.dot(t, wo_ref[...], preferred_element_type=F32)
    o_ref[...] = _post_block(x, s, p_ref[...].astype(BF16), g_ref[...], b_ref[...], wg_ref, wp_ref)


def _odd_layer(x, p, w_in_bf, vg, vb, mix_bf, bmap, wo_bf, g, b, wg_bf, wp_bf, *, tm, emit_v):
    m = x.shape[0]
    row = lambda w: pl.BlockSpec((tm, w), lambda i: (i, 0))
    col = lambda c: pl.BlockSpec((D_MODEL, W_C), lambda i, c=c: (0, c))
    out_shape = [jax.ShapeDtypeStruct((m, D_MODEL), F32)]
    out_specs = [row(D_MODEL)]
    if emit_v:
        out_shape.append(jax.ShapeDtypeStruct((m, W_C), F32))
        out_specs.append(row(W_C))
    return pl.pallas_call(
        functools.partial(_odd_kernel, tm=tm, emit_v=emit_v),
        grid=(m // tm,),
        in_specs=[row(D_MODEL), row(P_DIM), col(0), col(1), col(2),
                  _const_spec((1, W_C)), _const_spec((1, W_C)),
                  _const_spec(mix_bf.shape), _const_spec(bmap.shape), _const_spec(wo_bf.shape),
                  _const_spec((1, D_MODEL)), _const_spec((1, D_MODEL)),
                  _const_spec(wg_bf.shape), _const_spec(wp_bf.shape)],
        out_specs=out_specs,
        out_shape=out_shape,
        compiler_params=_cparams(("arbitrary",)),
        name="odd_layer_sample" if emit_v else "odd_layer_prompt",
    )(x, p, w_in_bf, w_in_bf, w_in_bf, vg.reshape(1, -1), vb.reshape(1, -1), mix_bf, bmap, wo_bf,
      g.reshape(1, -1), b.reshape(1, -1), wg_bf, wp_bf)


def _pick_tile(m, pref):
    t = min(pref, m)
    while m % t:
        t //= 2
    return t


def kernel(x_prompt, x_sample, cache_k, cache_v, state_conv, page_table, p_prompt, p_sample,
           w_in_e, conv_w, w_out_e, sb_bias, w_in_o, vnorm_g, vnorm_b, w_s, b_s, w_out_o,
           ln_g, ln_b, w_ple, w_ple_gate):
    bsz, t_p, _ = x_prompt.shape
    assert bsz == 1
    dec_b, n_tok, _ = x_sample.shape
    m_s = dec_b * n_tok
    assert m_s == CHUNK and t_p % SB_BLK == 0
    yp = x_prompt.reshape(t_p, D_MODEL)
    ys = x_sample.reshape(m_s, D_MODEL)
    outs = {}

    for i in range(DEPTH):
        li = i // 2
        wg_bf = w_ple_gate[i].astype(BF16)
        wp_bf = w_ple[i].astype(BF16)
        pp = p_prompt[i].reshape(t_p, P_DIM)
        ps = p_sample[i].reshape(m_s, P_DIM)
        if i % 2 == 0:
            w_in_bf = w_in_e[li].astype(BF16)
            wo_bf = w_out_e[li].astype(BF16)
            ya, q_s, k_bf, v_bf, gate_b, k32, v32, u_tail = _even_in(
                yp, w_in_bf, conv_w[li], tm=_pick_tile(t_p, 512))
            yb = _sb_prompt(q_s, k_bf, v_bf, gate_b, sb_bias[li])
            st = state_conv[li]
            zeros = jnp.zeros((dec_b, n_tok - 1, D_MODEL), F32)
            p1 = jnp.concatenate([st[:, 1:2], zeros], axis=1).reshape(m_s, D_MODEL)
            p2 = jnp.concatenate([st, zeros[:, 1:]], axis=1).reshape(m_s, D_MODEL)
            ya_s, q_ss, _, _, gate_s, k32_s, v32_s, u_s = _even_in(
                ys, w_in_bf, conv_w[li], (p1, p2), tm=m_s, seq_len=n_tok)
            yb_s = _sb_sample(q_ss, k32_s, v32_s, gate_s, sb_bias[li], cache_k[li], cache_v[li],
                              page_table, n_tok)
            yp = _out_post(yp, ya, yb, pp, wo_bf, ln_g[i], ln_b[i], wg_bf, wp_bf, tm=_pick_tile(t_p, 512))
            ys = _out_post(ys, ya_s, yb_s, ps, wo_bf, ln_g[i], ln_b[i], wg_bf, wp_bf, tm=m_s)
            outs.setdefault('kp', []).append(k32.reshape(bsz, t_p, N_HEADS, HEAD_DIM))
            outs.setdefault('vp', []).append(v32.reshape(bsz, t_p, N_HEADS, HEAD_DIM))
            outs.setdefault('ks', []).append(k32_s.reshape(dec_b, n_tok, N_HEADS, HEAD_DIM))
            outs.setdefault('vs', []).append(v32_s.reshape(dec_b, n_tok, N_HEADS, HEAD_DIM))
            outs.setdefault('cp', []).append(u_tail[8 - (CONV_W - 1):].reshape(bsz, CONV_W - 1, D_MODEL))
            outs.setdefault('cs', []).append(
                u_s.reshape(dec_b, n_tok, D_MODEL)[:, n_tok - (CONV_W - 1):])
        else:
            w_in_bf = w_in_o[li].astype(BF16)
            wo_bf = w_out_o[li].astype(BF16)
            tril = jnp.tril(w_s[li])
            bmap_p = jnp.repeat(b_s[li].T, W_C // G_C, axis=1)
            small = tril[:, :n_tok, :n_tok]
            mix_s = jnp.einsum('ab,gts->gatbs', jnp.eye(dec_b, dtype=F32), small).reshape(G_C, m_s, m_s)
            bmap_s = jnp.tile(bmap_p[:n_tok], (dec_b, 1))
            (yp,) = _odd_layer(yp, pp, w_in_bf, vnorm_g[li], vnorm_b[li], tril.astype(BF16), bmap_p, wo_bf,
                               ln_g[i], ln_b[i], wg_bf, wp_bf, tm=_pick_tile(t_p, 256), emit_v=False)
            ys, gv = _odd_layer(ys, ps, w_in_bf, vnorm_g[li], vnorm_b[li], mix_s.astype(BF16), bmap_s, wo_bf,
                                ln_g[i], ln_b[i], wg_bf, wp_bf, tm=m_s, emit_v=True)
            outs.setdefault('gv', []).append(gv.reshape(dec_b, n_tok, W_C))

    return (yp.reshape(bsz, t_p, D_MODEL), ys.reshape(dec_b, n_tok, D_MODEL),
            jnp.stack(outs['kp']), jnp.stack(outs['vp']), jnp.stack(outs['ks']), jnp.stack(outs['vs']),
            jnp.stack(outs['cp']), jnp.stack(outs['cs']), jnp.stack(outs['gv']))
```

```python
import functools
import math

import jax
import jax.numpy as jnp
from jax import lax
from jax.experimental import pallas as pl
from jax.experimental.pallas import tpu as pltpu

F32 = jnp.float32
BF16 = jnp.bfloat16

D_MODEL = 1024
HEAD_DIM = 64
N_HEADS = 16
CONV_W = 3
W_C = 2 * D_MODEL
G_C = 16
CHUNK = 128
P_DIM = 256
PAGE = 128
DEPTH = 2
ALPHA = (2 * DEPTH) ** 0.25
SB_SCALE = 1.0 / math.sqrt(HEAD_DIM)
LN_EPS = 1e-5
LOG2E = 1.4426950408889634

SB_BLK = 256
SB_PARTS = 2
SB_SEG = SB_BLK // (8 * SB_PARTS)
SB_BUFS = 2
HEADS_PER_STEP = 2
PAIR_W = HEADS_PER_STEP * HEAD_DIM

VMEM_LIMIT = 56 * 1024 * 1024


def _cparams(sem):
    return pltpu.CompilerParams(dimension_semantics=sem, vmem_limit_bytes=VMEM_LIMIT)


def _silu(x):
    return x * jax.nn.sigmoid(x)


def _layer_norm(x, g, b):
    mu = jnp.mean(x, axis=-1, keepdims=True)
    xc = x - mu
    var = jnp.mean(xc * xc, axis=-1, keepdims=True)
    return xc * lax.rsqrt(var + LN_EPS) * g + b


def _post_block(x, s, p_bf, g, b, wg_ref, wp_ref):
    h = _layer_norm(ALPHA * x + s, g, b)
    gate = jax.nn.sigmoid(jnp.dot(h.astype(BF16), wg_ref[...], preferred_element_type=F32))
    return h + gate * jnp.dot(p_bf, wp_ref[...], preferred_element_type=F32)


def _even_in_kernel(*refs, tm, cw, seq_len):
    x_ref = refs[0]
    wh, wb, wc, wza, wq, wk, wv, wzb = refs[1:9]
    cw_ref = refs[9]
    pos = 10
    if seq_len is not None:
        p1_ref, p2_ref = refs[pos], refs[pos + 1]
        pos += 2
    ya_ref, q_ref, kb_ref, vb_ref, zb_ref, k32_ref, v32_ref, u_ref = refs[pos:pos + 8]
    pos += 8
    carry_ref = refs[pos] if seq_len is None else None

    xb = x_ref[...].astype(BF16)

    def proj(w_ref):
        return jnp.dot(xb, w_ref[...], preferred_element_type=F32)

    u = proj(wc) * proj(wh)
    row = lax.broadcasted_iota(jnp.int32, (tm, cw), 0)
    r1 = pltpu.roll(u, 1, 0)
    r2 = pltpu.roll(u, 2, 0)
    if seq_len is None:
        @pl.when(pl.program_id(1) == 0)
        def _():
            carry_ref[...] = jnp.zeros_like(carry_ref)

        c2 = carry_ref[6:7, :]
        c1 = carry_ref[7:8, :]
        prev1 = jnp.where(row >= 1, r1, c1)
        prev2 = jnp.where(row >= 2, r2, jnp.where(row == 0, c2, c1))
        carry_ref[...] = u[tm - 8:, :]
        u_ref[...] = u[tm - 8:, :]
    else:
        t_in = row % seq_len
        prev1 = jnp.where(t_in >= 1, r1, p1_ref[...])
        prev2 = jnp.where(t_in >= 2, r2, p2_ref[...])
        u_ref[...] = u
    conv = cw_ref[0:1, :] * prev2 + cw_ref[1:2, :] * prev1 + cw_ref[2:3, :] * u
    ya_ref[...] = (proj(wb) * conv * _silu(proj(wza))).astype(BF16)

    q_ref[...] = (proj(wq) * (0.5 * SB_SCALE)).astype(BF16)
    k = proj(wk)
    k32_ref[...] = k
    kb_ref[...] = k.astype(BF16)
    v = proj(wv)
    v32_ref[...] = v
    seg_pos = (row % SB_SEG).astype(F32)
    vb_ref[...] = (v * jnp.exp2(seg_pos - float(SB_SEG))).astype(BF16)
    zb_ref[...] = _silu(proj(wzb)).astype(BF16)


def _even_in(x, w_bf, conv_w, prefix=None, *, tm, seq_len=None, cw=256):
    m = x.shape[0]
    ncb = D_MODEL // cw
    grid = (ncb, m // tm)
    in_specs = [pl.BlockSpec((tm, D_MODEL), lambda j, i: (i, 0))]
    for c in range(8):
        in_specs.append(pl.BlockSpec((D_MODEL, cw), lambda j, i, c=c: (0, c * ncb + j)))
    in_specs.append(pl.BlockSpec((CONV_W, cw), lambda j, i: (0, j)))
    args = [x] + [w_bf] * 8 + [conv_w]
    if seq_len is not None:
        in_specs += [pl.BlockSpec((tm, cw), lambda j, i: (i, j))] * 2
        args += list(prefix)
    tile = pl.BlockSpec((tm, cw), lambda j, i: (i, j))
    u_rows = 8 if seq_len is None else m
    u_spec = (pl.BlockSpec((8, cw), lambda j, i: (0, j)) if seq_len is None else tile)
    out_shape = ([jax.ShapeDtypeStruct((m, D_MODEL), BF16)] * 5
                 + [jax.ShapeDtypeStruct((m, D_MODEL), F32)] * 2
                 + [jax.ShapeDtypeStruct((u_rows, D_MODEL), F32)])
    out_specs = [tile] * 7 + [u_spec]
    scratch = [pltpu.VMEM((8, cw), F32)] if seq_len is None else []
    return pl.pallas_call(
        functools.partial(_even_in_kernel, tm=tm, cw=cw, seq_len=seq_len),
        grid=grid, in_specs=in_specs, out_specs=out_specs, out_shape=out_shape,
        scratch_shapes=scratch,
        compiler_params=_cparams(("arbitrary", "arbitrary")),
        name="even_in_prompt" if seq_len is None else "even_in_sample",
    )(*args)


def _sb_prompt_kernel(hbias_ref, qT_ref, k_ref, vT_ref, zb_ref, o_ref, *scratch):
    z_refs = scratch[:SB_BUFS]
    a_refs = scratch[SB_BUFS:2 * SB_BUFS]
    t_refs = scratch[2 * SB_BUFS:2 * SB_BUFS + 2]
    acc_ref = scratch[2 * SB_BUFS + 2]
    qi = pl.program_id(1)
    w2 = HEADS_PER_STEP * SB_BLK
    part = 8 * SB_SEG
    qT = qT_ref[0]
    hrow = lax.broadcasted_iota(jnp.int32, (PAIR_W, w2), 0)
    hcol = lax.broadcasted_iota(jnp.int32, (PAIR_W, w2), 1) // SB_BLK
    qq = jnp.concatenate([qT.astype(F32)] * HEADS_PER_STEP, axis=1)
    qT2 = jnp.where(hrow // HEAD_DIM == hcol, qq, 0.0)
    hb = hbias_ref[0][0:1, :]
    b_hi = hb.astype(BF16).astype(F32)
    b_mid = (hb - b_hi).astype(BF16).astype(F32)
    b_lo = (hb - b_hi - b_mid).astype(BF16).astype(F32)
    brow = jnp.where(hrow == 0, b_hi, jnp.where(hrow == 1, b_mid, jnp.where(hrow == 2, b_lo, 0.0)))
    q_aug = jnp.concatenate([qT2, brow], axis=0).astype(BF16)
    ones_cols = (lax.broadcasted_iota(jnp.int32, (SB_BLK, PAIR_W), 1) < 3).astype(BF16)
    sub = lax.broadcasted_iota(jnp.int32, (8, w2), 0)
    lane = lax.broadcasted_iota(jnp.int32, (8, w2), 1)
    diag = (lane % SB_BLK) - sub * SB_SEG
    acc_ref[...] = jnp.zeros_like(acc_ref)

    def scores(kj, z_ref):
        kj = jnp.maximum(kj, 0)
        kb = k_ref[pl.ds(pl.multiple_of(kj * SB_BLK, SB_BLK), SB_BLK), :]
        k_aug = jnp.concatenate([kb, ones_cols], axis=1)
        z_ref[...] = jnp.dot(k_aug, q_aug, preferred_element_type=F32)

    def later_segments(run):
        x0 = jnp.where(sub < 7, pltpu.roll(run, 7, 0), 1.0)
        s1 = x0 * jnp.where(sub < 7, pltpu.roll(x0, 7, 0), 1.0)
        s2 = s1 * jnp.where(sub < 6, pltpu.roll(s1, 6, 0), 1.0)
        return s2 * jnp.where(sub < 4, pltpu.roll(s2, 4, 0), 1.0)

    def weights(z_ref, a_ref, t_ref, carry, masked):
        run2 = [jnp.ones((8, w2), F32) for _ in range(SB_PARTS)]
        held = [None] * SB_PARTS
        for r in reversed(range(SB_SEG)):
            for h in range(SB_PARTS):
                row0 = h * part + r * 8
                th = jnp.tanh(z_ref[row0:row0 + 8, :])
                if masked:
                    th = jnp.where(diag > h * part + r, th, -1.0)
                p = run2[h] * th
                if r % 2:
                    held[h] = run2[h] + p
                else:
                    t_ref[row0:row0 + 16, :] = jnp.concatenate([run2[h] + p, held[h]], axis=0).astype(BF16)
                run2[h] = run2[h] - p
        for h in reversed(range(SB_PARTS)):
            run = run2[h] * (2.0 ** -SB_SEG)
            off = later_segments(run)
            base = off * carry
            base16 = jnp.concatenate([base, base], axis=0).astype(BF16)
            t3 = t_ref[h * part:(h + 1) * part, :].reshape(part // 16, 16, w2)
            a_ref[h * part:(h + 1) * part, :] = (t3 * base16[None]).reshape(part, w2)
            carry = carry * (off[0:1, :] * run[0:1, :])
        return carry

    def values(kj, a_ref):
        vt = vT_ref[0, jnp.maximum(kj, 0)]
        vt = jnp.where(kj >= 0, vt, jnp.zeros_like(vt))
        acc_ref[...] += jnp.dot(vt, a_ref[...], preferred_element_type=F32)

    for b in range(SB_BUFS):
        scores(qi - b, z_refs[b])
    carry = weights(z_refs[0], a_refs[0], t_refs[0], jnp.ones((8, w2), F32), True)

    def group(i, carry):
        for s in range(SB_BUFS):
            kj = qi - (SB_BUFS * i + 1 + s)
            cur = (1 + s) % SB_BUFS
            scores(kj - (SB_BUFS - 1), z_refs[s])
            values(kj + 1, a_refs[s])
            carry = weights(z_refs[cur], a_refs[cur], t_refs[s % 2], carry, False)
        return carry

    n_groups = (qi + SB_BUFS - 1) // SB_BUFS
    lax.fori_loop(0, n_groups, group, carry)
    values(qi - SB_BUFS * n_groups, a_refs[0])

    oT = jnp.concatenate([acc_ref[:HEAD_DIM, :SB_BLK], acc_ref[HEAD_DIM:, SB_BLK:]], axis=0)
    o_ref[...] = (oT.T * zb_ref[...].astype(F32)).astype(BF16)


def _sb_prompt(q_s, k_bf, v_bf, gate_b, sb_bias):
    t = q_s.shape[0]
    nb = t // SB_BLK
    npair = N_HEADS // HEADS_PER_STEP
    w2 = HEADS_PER_STEP * SB_BLK
    qT = q_s.reshape(t, npair, PAIR_W).transpose(1, 2, 0)
    k_perm = k_bf.reshape(nb, SB_PARTS, 8, SB_SEG, D_MODEL).transpose(0, 1, 3, 2, 4).reshape(t, D_MODEL)
    vT = v_bf.reshape(nb, SB_PARTS, 8, SB_SEG, npair, PAIR_W).transpose(4, 0, 5, 1, 3, 2)
    vT = vT.reshape(npair, nb, PAIR_W, SB_BLK)
    hbias = jnp.repeat(0.5 * sb_bias.astype(F32), SB_BLK).reshape(npair, 1, w2)
    hbias = jnp.broadcast_to(hbias, (npair, 8, w2))
    return pl.pallas_call(
        _sb_prompt_kernel,
        grid=(npair, nb),
        in_specs=[
            pl.BlockSpec((1, 8, w2), lambda p, i: (p, 0, 0)),
            pl.BlockSpec((1, PAIR_W, SB_BLK), lambda p, i: (p, 0, i)),
            pl.BlockSpec((t, PAIR_W), lambda p, i: (0, p)),
            pl.BlockSpec((1, nb, PAIR_W, SB_BLK), lambda p, i: (p, 0, 0, 0)),
            pl.BlockSpec((SB_BLK, PAIR_W), lambda p, i: (i, p)),
        ],
        out_specs=pl.BlockSpec((SB_BLK, PAIR_W), lambda p, i: (i, p)),
        out_shape=jax.ShapeDtypeStruct((t, D_MODEL), BF16),
        scratch_shapes=([pltpu.VMEM((SB_BLK, w2), F32)] * SB_BUFS
                        + [pltpu.VMEM((SB_BLK, w2), BF16)] * SB_BUFS
                        + [pltpu.VMEM((SB_BLK, w2), BF16)] * 2
                        + [pltpu.VMEM((PAIR_W, w2), F32)]),
        compiler_params=_cparams(("arbitrary", "arbitrary")),
        name="sb_prompt",
    )(hbias, qT, k_perm, vT, gate_b)


SAMPLE_PAGES_PER_STEP = 8
SAMPLE_UNIT_PAGES = 2


def _sb_sample_kernel(pt_ref, qbd_ref, bias_ref, knew_ref, vnew_ref, zb_ref, *refs, n_tok):
    pg = SAMPLE_PAGES_PER_STEP
    k_refs = refs[:pg]
    v_refs = refs[pg:2 * pg]
    o_ref = refs[2 * pg]
    acc_ref, carry_ref = refs[2 * pg + 1:]
    j = pl.program_id(1)
    nrow = n_tok * N_HEADS
    unit = SAMPLE_UNIT_PAGES * PAGE
    qbd = qbd_ref[0]
    bias = bias_ref[...]
    ri = lax.broadcasted_iota(jnp.int32, (unit, unit), 0)
    ci = lax.broadcasted_iota(jnp.int32, (unit, unit), 1)
    later = (ri > ci).astype(BF16)

    def pages(p_refs, first, count):
        return jnp.concatenate([p_refs[first + n][0].astype(BF16) for n in range(count)], axis=1)

    def local(kT_bf, keep):
        z = jnp.dot(qbd, kT_bf, preferred_element_type=F32) + bias
        lse = jnp.log2(1.0 + jnp.exp2(-jnp.abs(z)))
        lb = jnp.minimum(z, 0.0) - lse
        lr = lb - z
        if keep is not None:
            lr = jnp.where(keep, lr, 0.0)
        hi = lr.astype(BF16)
        lo = (lr - hi.astype(F32)).astype(BF16)
        after = (jnp.dot(hi, later, preferred_element_type=F32)
                 + jnp.dot(lo, later, preferred_element_type=F32))
        return lb + after, jnp.sum(lr, axis=1, keepdims=True)

    def weighted(pre, carry, vT_bf, keep):
        a = jnp.exp2(pre + carry)
        if keep is not None:
            a = jnp.where(keep, a, 0.0)
        return lax.dot_general(a.astype(BF16), vT_bf, (((1,), (1,)), ((), ())),
                               preferred_element_type=F32)

    @pl.when(j == 0)
    def _():
        rr = lax.broadcasted_iota(jnp.int32, (nrow, unit), 0)
        cc = lax.broadcasted_iota(jnp.int32, (nrow, unit), 1)
        keep = jnp.logical_and(cc < rr // N_HEADS, cc < n_tok)
        pre, tot = local(knew_ref[0], keep)
        acc_ref[...] = weighted(pre, 0.0, vnew_ref[0], keep)
        carry_ref[...] = jnp.broadcast_to(tot, carry_ref.shape)

    n_unit = pg // SAMPLE_UNIT_PAGES
    z = jnp.dot(qbd, pages(k_refs, 0, pg), preferred_element_type=F32)
    z = z + jnp.concatenate([bias] * n_unit, axis=1)
    lse = jnp.log2(1.0 + jnp.exp2(-jnp.abs(z)))
    lb = jnp.minimum(z, 0.0) - lse
    lr = lb - z
    cols = [slice(u * unit, (u + 1) * unit) for u in range(n_unit)]
    lr_rows = jnp.concatenate([lr[:, c] for c in cols], axis=0)
    hi = lr_rows.astype(BF16)
    lo = (lr_rows - hi.astype(F32)).astype(BF16)
    both = jnp.dot(jnp.concatenate([hi, lo], axis=0), later, preferred_element_type=F32)
    after = both[:n_unit * nrow] + both[n_unit * nrow:]
    carry = carry_ref[...]
    a_cols = [None] * n_unit
    for u in reversed(range(n_unit)):
        a_cols[u] = jnp.exp2(lb[:, cols[u]] + after[u * nrow:(u + 1) * nrow] + carry)
        carry = carry + jnp.sum(lr[:, cols[u]], axis=1, keepdims=True)
    a = jnp.concatenate(a_cols, axis=1).astype(BF16)
    acc_ref[...] += lax.dot_general(a, pages(v_refs, 0, pg), (((1,), (1,)), ((), ())),
                                    preferred_element_type=F32)
    carry_ref[...] = carry

    @pl.when(j == pl.num_programs(1) - 1)
    def _():
        hcol = lax.broadcasted_iota(jnp.int32, (N_HEADS, D_MODEL), 1) // HEAD_DIM
        hrow = lax.broadcasted_iota(jnp.int32, (N_HEADS, D_MODEL), 0)
        sel = hcol == hrow
        rows = []
        for t in range(n_tok):
            blk = acc_ref[t * N_HEADS:(t + 1) * N_HEADS, :]
            rows.append(jnp.sum(jnp.where(sel, blk, 0.0), axis=0, keepdims=True))
        o = jnp.concatenate(rows, axis=0)
        o_ref[0] = (o * zb_ref[0].astype(F32)).astype(BF16)


def _sb_sample(q_s, k_new, v_new, gate_b, sb_bias, cache_k, cache_v, page_table, n_tok):
    dec_b, n_pages = page_table.shape
    n_pool = cache_k.shape[0]
    pg = SAMPLE_PAGES_PER_STEP
    npg = n_pages // pg
    nrow = n_tok * N_HEADS
    unit = SAMPLE_UNIT_PAGES * PAGE
    ck = cache_k.transpose(0, 2, 3, 1).reshape(n_pool, D_MODEL, PAGE)
    cv = cache_v.transpose(0, 2, 3, 1).reshape(n_pool, D_MODEL, PAGE)
    q4 = (q_s.astype(F32) * (2.0 * LOG2E)).reshape(dec_b, n_tok, N_HEADS, HEAD_DIM)
    eye = jnp.eye(N_HEADS, dtype=F32)
    qbd = jnp.einsum('bthd,hg->bthgd', q4, eye).reshape(dec_b, nrow, D_MODEL).astype(BF16)
    bias = jnp.broadcast_to(jnp.tile(LOG2E * sb_bias.astype(F32), n_tok)[:, None], (nrow, unit))
    pad_keys = ((0, 0), (0, 0), (0, unit - n_tok))
    knew = jnp.pad(k_new.reshape(dec_b, n_tok, D_MODEL).transpose(0, 2, 1), pad_keys).astype(BF16)
    vnew = jnp.pad(v_new.reshape(dec_b, n_tok, D_MODEL).transpose(0, 2, 1), pad_keys).astype(BF16)
    gate3 = gate_b.reshape(dec_b, n_tok, D_MODEL)

    def page_spec(i):
        return pl.BlockSpec((1, D_MODEL, PAGE), lambda b, j, pt, i=i: (pt[b, pg * (npg - 1 - j) + i], 0, 0))

    per_seq = lambda shape: pl.BlockSpec(shape, lambda b, j, pt: (b, 0, 0))
    grid_spec = pltpu.PrefetchScalarGridSpec(
        num_scalar_prefetch=1,
        grid=(dec_b, npg),
        in_specs=[per_seq((1, nrow, D_MODEL)),
                  pl.BlockSpec((nrow, unit), lambda b, j, pt: (0, 0)),
                  per_seq((1, D_MODEL, unit)), per_seq((1, D_MODEL, unit)),
                  per_seq((1, n_tok, D_MODEL))]
                 + [page_spec(i) for i in range(pg)] * 2,
        out_specs=per_seq((1, n_tok, D_MODEL)),
        scratch_shapes=[pltpu.VMEM((nrow, D_MODEL), F32), pltpu.VMEM((nrow, unit), F32)],
    )
    out = pl.pallas_call(
        functools.partial(_sb_sample_kernel, n_tok=n_tok),
        grid_spec=grid_spec,
        out_shape=jax.ShapeDtypeStruct((dec_b, n_tok, D_MODEL), BF16),
        compiler_params=_cparams(("arbitrary", "arbitrary")),
        name="sb_sample",
    )(page_table, qbd, bias, knew, vnew, gate3, *([ck] * pg), *([cv] * pg))
    return out.reshape(dec_b * n_tok, D_MODEL)


def _out_post_kernel(x_ref, ya_ref, yb_ref, p_ref, wo_ref, g_ref, b_ref, wg_ref, wp_ref, o_ref):
    s = (jnp.dot(ya_ref[...], wo_ref[:D_MODEL, :], preferred_element_type=F32)
         + jnp.dot(yb_ref[...], wo_ref[D_MODEL:, :], preferred_element_type=F32))
    o_ref[...] = _post_block(x_ref[...], s, p_ref[...].astype(BF16), g_ref[...], b_ref[...], wg_ref, wp_ref)


def _const_spec(shape):
    nd = len(shape)
    return pl.BlockSpec(shape, lambda i: (0,) * nd)


def _out_post(x, ya, yb, p, wo_bf, g, b, wg_bf, wp_bf, *, tm):
    m = x.shape[0]
    row = lambda w: pl.BlockSpec((tm, w), lambda i: (i, 0))
    return pl.pallas_call(
        _out_post_kernel,
        grid=(m // tm,),
        in_specs=[row(D_MODEL), row(D_MODEL), row(D_MODEL), row(P_DIM),
                  _const_spec(wo_bf.shape), _const_spec((1, D_MODEL)), _const_spec((1, D_MODEL)),
                  _const_spec(wg_bf.shape), _const_spec(wp_bf.shape)],
        out_specs=row(D_MODEL),
        out_shape=jax.ShapeDtypeStruct((m, D_MODEL), F32),
        compiler_params=_cparams(("arbitrary",)),
        name="out_post",
    )(x, ya, yb, p, wo_bf, g.reshape(1, -1), b.reshape(1, -1), wg_bf, wp_bf)


def _odd_kernel(x_ref, p_ref, wu_ref, wv_ref, wz_ref, vg_ref, vb_ref, mix_ref, bmap_ref, wo_ref,
                g_ref, b_ref, wg_ref, wp_ref, *out_refs, tm, emit_v):
    o_ref = out_refs[0]
    x = x_ref[...]
    xb = x.astype(BF16)
    v = jax.nn.gelu(jnp.dot(xb, wv_ref[...], preferred_element_type=F32), approximate=True)
    vn = _layer_norm(v, vg_ref[...], vb_ref[...])
    if emit_v:
        out_refs[1][...] = vn
    vnb = vn.astype(BF16)
    chunks = []
    for c in range(tm // CHUNK):
        cols = []
        for g in range(G_C):
            blk = vnb[c * CHUNK:(c + 1) * CHUNK, g * CHUNK:(g + 1) * CHUNK]
            cols.append(jnp.dot(mix_ref[g], blk, preferred_element_type=F32))
        chunks.append(jnp.concatenate(cols, axis=1) + bmap_ref[...])
    mixed = chunks[0] if len(chunks) == 1 else jnp.concatenate(chunks, axis=0)
    u = jax.nn.gelu(jnp.dot(xb, wu_ref[...], preferred_element_type=F32), approximate=True)
    zz = _silu(jnp.dot(xb, wz_ref[...], preferred_element_type=F32))
    t = (u * mixed * zz).astype(BF16)
    s = jnp.dot(t, wo_ref[...], preferred_element_type=F32)
    o_ref[...] = _post_block(x, s, p_ref[...].astype(BF16), g_ref[...], b_ref[...], wg_ref, wp_ref)


def _odd_layer(x, p, w_in_bf, vg, vb, mix_bf, bmap, wo_bf, g, b, wg_bf, wp_bf, *, tm, emit_v):
    m = x.shape[0]
    row = lambda w: pl.BlockSpec((tm, w), lambda i: (i, 0))
    col = lambda c: pl.BlockSpec((D_MODEL, W_C), lambda i, c=c: (0, c))
    out_shape = [jax.ShapeDtypeStruct((m, D_MODEL), F32)]
    out_specs = [row(D_MODEL)]
    if emit_v:
        out_shape.append(jax.ShapeDtypeStruct((m, W_C), F32))
        out_specs.append(row(W_C))
    return pl.pallas_call(
        functools.partial(_odd_kernel, tm=tm, emit_v=emit_v),
        grid=(m // tm,),
        in_specs=[row(D_MODEL), row(P_DIM), col(0), col(1), col(2),
                  _const_spec((1, W_C)), _const_spec((1, W_C)),
                  _const_spec(mix_bf.shape), _const_spec(bmap.shape), _const_spec(wo_bf.shape),
                  _const_spec((1, D_MODEL)), _const_spec((1, D_MODEL)),
                  _const_spec(wg_bf.shape), _const_spec(wp_bf.shape)],
        out_specs=out_specs,
        out_shape=out_shape,
        compiler_params=_cparams(("arbitrary",)),
        name="odd_layer_sample" if emit_v else "odd_layer_prompt",
    )(x, p, w_in_bf, w_in_bf, w_in_bf, vg.reshape(1, -1), vb.reshape(1, -1), mix_bf, bmap, wo_bf,
      g.reshape(1, -1), b.reshape(1, -1), wg_bf, wp_bf)


def _pick_tile(m, pref):
    t = min(pref, m)
    while m % t:
        t //= 2
    return t


def kernel(x_prompt, x_sample, cache_k, cache_v, state_conv, page_table, p_prompt, p_sample,
           w_in_e, conv_w, w_out_e, sb_bias, w_in_o, vnorm_g, vnorm_b, w_s, b_s, w_out_o,
           ln_g, ln_b, w_ple, w_ple_gate):
    bsz, t_p, _ = x_prompt.shape
    assert bsz == 1
    dec_b, n_tok, _ = x_sample.shape
    m_s = dec_b * n_tok
    assert m_s == CHUNK and t_p % SB_BLK == 0
    yp = x_prompt.reshape(t_p, D_MODEL)
    ys = x_sample.reshape(m_s, D_MODEL)
    outs = {}

    for i in range(DEPTH):
        li = i // 2
        wg_bf = w_ple_gate[i].astype(BF16)
        wp_bf = w_ple[i].astype(BF16)
        pp = p_prompt[i].reshape(t_p, P_DIM)
        ps = p_sample[i].reshape(m_s, P_DIM)
        if i % 2 == 0:
            w_in_bf = w_in_e[li].astype(BF16)
            wo_bf = w_out_e[li].astype(BF16)
            ya, q_s, k_bf, v_bf, gate_b, k32, v32, u_tail = _even_in(
                yp, w_in_bf, conv_w[li], tm=_pick_tile(t_p, 512))
            yb = _sb_prompt(q_s, k_bf, v_bf, gate_b, sb_bias[li])
            st = state_conv[li]
            zeros = jnp.zeros((dec_b, n_tok - 1, D_MODEL), F32)
            p1 = jnp.concatenate([st[:, 1:2], zeros], axis=1).reshape(m_s, D_MODEL)
            p2 = jnp.concatenate([st, zeros[:, 1:]], axis=1).reshape(m_s, D_MODEL)
            ya_s, q_ss, _, _, gate_s, k32_s, v32_s, u_s = _even_in(
                ys, w_in_bf, conv_w[li], (p1, p2), tm=m_s, seq_len=n_tok)
            yb_s = _sb_sample(q_ss, k32_s, v32_s, gate_s, sb_bias[li], cache_k[li], cache_v[li],
                              page_table, n_tok)
            yp = _out_post(yp, ya, yb, pp, wo_bf, ln_g[i], ln_b[i], wg_bf, wp_bf, tm=_pick_tile(t_p, 512))
            ys = _out_post(ys, ya_s, yb_s, ps, wo_bf, ln_g[i], ln_b[i], wg_bf, wp_bf, tm=m_s)
            outs.setdefault('kp', []).append(k32.reshape(bsz, t_p, N_HEADS, HEAD_DIM))
            outs.setdefault('vp', []).append(v32.reshape(bsz, t_p, N_HEADS, HEAD_DIM))
            outs.setdefault('ks', []).append(k32_s.reshape(dec_b, n_tok, N_HEADS, HEAD_DIM))
            outs.setdefault('vs', []).append(v32_s.reshape(dec_b, n_tok, N_HEADS, HEAD_DIM))
            outs.setdefault('cp', []).append(u_tail[8 - (CONV_W - 1):].reshape(bsz, CONV_W - 1, D_MODEL))
            outs.setdefault('cs', []).append(
                u_s.reshape(dec_b, n_tok, D_MODEL)[:, n_tok - (CONV_W - 1):])
        else:
            w_in_bf = w_in_o[li].astype(BF16)
            wo_bf = w_out_o[li].astype(BF16)
            tril = jnp.tril(w_s[li])
            bmap_p = jnp.repeat(b_s[li].T, W_C // G_C, axis=1)
            small = tril[:, :n_tok, :n_tok]
            mix_s = jnp.einsum('ab,gts->gatbs', jnp.eye(dec_b, dtype=F32), small).reshape(G_C, m_s, m_s)
            bmap_s = jnp.tile(bmap_p[:n_tok], (dec_b, 1))
            (yp,) = _odd_layer(yp, pp, w_in_bf, vnorm_g[li], vnorm_b[li], tril.astype(BF16), bmap_p, wo_bf,
                               ln_g[i], ln_b[i], wg_bf, wp_bf, tm=_pick_tile(t_p, 256), emit_v=False)
            ys, gv = _odd_layer(ys, ps, w_in_bf, vnorm_g[li], vnorm_b[li], mix_s.astype(BF16), bmap_s, wo_bf,
                                ln_g[i], ln_b[i], wg_bf, wp_bf, tm=m_s, emit_v=True)
            outs.setdefault('gv', []).append(gv.reshape(dec_b, n_tok, W_C))

    return (yp.reshape(bsz, t_p, D_MODEL), ys.reshape(dec_b, n_tok, D_MODEL),
            jnp.stack(outs['kp']), jnp.stack(outs['vp']), jnp.stack(outs['ks']), jnp.stack(outs['vs']),
            jnp.stack(outs['cp']), jnp.stack(outs['cs']), jnp.stack(outs['gv']))
```

```python
import functools
import math

import jax
import jax.numpy as jnp
from jax import lax
from jax.experimental import pallas as pl
from jax.experimental.pallas import tpu as pltpu

F32 = jnp.float32
BF16 = jnp.bfloat16

D_MODEL = 1024
HEAD_DIM = 64
N_HEADS = 16
CONV_W = 3
W_C = 2 * D_MODEL
G_C = 16
CHUNK = 128
P_DIM = 256
PAGE = 128
DEPTH = 2
ALPHA = (2 * DEPTH) ** 0.25
SB_SCALE = 1.0 / math.sqrt(HEAD_DIM)
LN_EPS = 1e-5
LOG2E = 1.4426950408889634

SB_BLK = 256
SB_PARTS = 2
SB_SEG = SB_BLK // (8 * SB_PARTS)
SB_QBLK = 512
SB_LANES = 512
HEADS_PER_STEP = 2
PAIR_W = HEADS_PER_STEP * HEAD_DIM

VMEM_LIMIT = 56 * 1024 * 1024


def _cparams(sem):
    return pltpu.CompilerParams(dimension_semantics=sem, vmem_limit_bytes=VMEM_LIMIT)


def _silu(x):
    return x * jax.nn.sigmoid(x)


def _layer_norm(x, g, b):
    mu = jnp.mean(x, axis=-1, keepdims=True)
    xc = x - mu
    var = jnp.mean(xc * xc, axis=-1, keepdims=True)
    return xc * lax.rsqrt(var + LN_EPS) * g + b


def _post_block(x, s, p_bf, g, b, wg_ref, wp_ref):
    h = _layer_norm(ALPHA * x + s, g, b)
    gate = jax.nn.sigmoid(jnp.dot(h.astype(BF16), wg_ref[...], preferred_element_type=F32))
    return h + gate * jnp.dot(p_bf, wp_ref[...], preferred_element_type=F32)


def _even_in_kernel(*refs, tm, cw, seq_len):
    x_ref = refs[0]
    wh, wb, wc, wza, wq, wk, wv, wzb = refs[1:9]
    cw_ref = refs[9]
    pos = 10
    if seq_len is not None:
        p1_ref, p2_ref = refs[pos], refs[pos + 1]
        pos += 2
    ya_ref, q_ref, kb_ref, vb_ref, zb_ref, k32_ref, v32_ref, u_ref = refs[pos:pos + 8]
    pos += 8
    carry_ref = refs[pos] if seq_len is None else None

    xb = x_ref[...].astype(BF16)

    def proj(w_ref):
        return jnp.dot(xb, w_ref[...], preferred_element_type=F32)

    u = proj(wc) * proj(wh)
    row = lax.broadcasted_iota(jnp.int32, (tm, cw), 0)
    r1 = pltpu.roll(u, 1, 0)
    r2 = pltpu.roll(u, 2, 0)
    if seq_len is None:
        @pl.when(pl.program_id(1) == 0)
        def _():
            carry_ref[...] = jnp.zeros_like(carry_ref)

        c2 = carry_ref[6:7, :]
        c1 = carry_ref[7:8, :]
        prev1 = jnp.where(row >= 1, r1, c1)
        prev2 = jnp.where(row >= 2, r2, jnp.where(row == 0, c2, c1))
        carry_ref[...] = u[tm - 8:, :]
        u_ref[...] = u[tm - 8:, :]
    else:
        t_in = row % seq_len
        prev1 = jnp.where(t_in >= 1, r1, p1_ref[...])
        prev2 = jnp.where(t_in >= 2, r2, p2_ref[...])
        u_ref[...] = u
    conv = cw_ref[0:1, :] * prev2 + cw_ref[1:2, :] * prev1 + cw_ref[2:3, :] * u
    ya_ref[...] = (proj(wb) * conv * _silu(proj(wza))).astype(BF16)

    q_ref[...] = (proj(wq) * (0.5 * SB_SCALE)).astype(BF16)
    k = proj(wk)
    k32_ref[...] = k
    kb_ref[...] = k.astype(BF16)
    v = proj(wv)
    v32_ref[...] = v
    seg_pos = (row % SB_SEG).astype(F32)
    vb_ref[...] = (v * jnp.exp2(seg_pos - float(SB_SEG))).astype(BF16)
    zb_ref[...] = _silu(proj(wzb)).astype(BF16)


def _even_in(x, w_bf, conv_w, prefix=None, *, tm, seq_len=None, cw=256):
    m = x.shape[0]
    ncb = D_MODEL // cw
    grid = (ncb, m // tm)
    in_specs = [pl.BlockSpec((tm, D_MODEL), lambda j, i: (i, 0))]
    for c in range(8):
        in_specs.append(pl.BlockSpec((D_MODEL, cw), lambda j, i, c=c: (0, c * ncb + j)))
    in_specs.append(pl.BlockSpec((CONV_W, cw), lambda j, i: (0, j)))
    args = [x] + [w_bf] * 8 + [conv_w]
    if seq_len is not None:
        in_specs += [pl.BlockSpec((tm, cw), lambda j, i: (i, j))] * 2
        args += list(prefix)
    tile = pl.BlockSpec((tm, cw), lambda j, i: (i, j))
    u_rows = 8 if seq_len is None else m
    u_spec = (pl.BlockSpec((8, cw), lambda j, i: (0, j)) if seq_len is None else tile)
    out_shape = ([jax.ShapeDtypeStruct((m, D_MODEL), BF16)] * 5
                 + [jax.ShapeDtypeStruct((m, D_MODEL), F32)] * 2
                 + [jax.ShapeDtypeStruct((u_rows, D_MODEL), F32)])
    out_specs = [tile] * 7 + [u_spec]
    scratch = [pltpu.VMEM((8, cw), F32)] if seq_len is None else []
    return pl.pallas_call(
        functools.partial(_even_in_kernel, tm=tm, cw=cw, seq_len=seq_len),
        grid=grid, in_specs=in_specs, out_specs=out_specs, out_shape=out_shape,
        scratch_shapes=scratch,
        compiler_params=_cparams(("arbitrary", "arbitrary")),
        name="even_in_prompt" if seq_len is None else "even_in_sample",
    )(*args)


def _sb_prompt_kernel(hbias_ref, q_ref, k_ref, vT_ref, zb_ref, o_ref, *scratch):
    z_refs = scratch[0:2]
    a_refs = scratch[2:4]
    t_refs = scratch[4:6]
    acc_ref = scratch[6]
    qi = pl.program_id(1)
    kq = SB_QBLK // SB_BLK
    top = kq * qi + (kq - 1)
    w2 = HEADS_PER_STEP * SB_QBLK
    part = 8 * SB_SEG
    qT = q_ref[...].astype(F32).T
    hrow = lax.broadcasted_iota(jnp.int32, (PAIR_W, w2), 0)
    hcol = lax.broadcasted_iota(jnp.int32, (PAIR_W, w2), 1) // SB_QBLK
    qq = jnp.concatenate([qT] * HEADS_PER_STEP, axis=1)
    qT2 = jnp.where(hrow // HEAD_DIM == hcol, qq, 0.0)
    hb = hbias_ref[0][0:1, :]
    b_hi = hb.astype(BF16).astype(F32)
    b_mid = (hb - b_hi).astype(BF16).astype(F32)
    b_lo = (hb - b_hi - b_mid).astype(BF16).astype(F32)
    brow = jnp.where(hrow == 0, b_hi, jnp.where(hrow == 1, b_mid, jnp.where(hrow == 2, b_lo, 0.0)))
    q_aug = jnp.concatenate([qT2, brow], axis=0).astype(BF16)
    ones_cols = (lax.broadcasted_iota(jnp.int32, (SB_BLK, PAIR_W), 1) < 3).astype(BF16)
    sub = lax.broadcasted_iota(jnp.int32, (8, SB_LANES), 0)
    lane = lax.broadcasted_iota(jnp.int32, (8, SB_LANES), 1)
    diags = [((lane + c0) % SB_QBLK) - sub * SB_SEG for c0 in range(0, w2, SB_LANES)]
    acc_ref[...] = jnp.zeros_like(acc_ref)

    def scores(kj, z_ref):
        kj = jnp.maximum(kj, 0)
        kb = k_ref[pl.ds(pl.multiple_of(kj * SB_BLK, SB_BLK), SB_BLK), :]
        k_aug = jnp.concatenate([kb, ones_cols], axis=1)
        z_ref[...] = jnp.dot(k_aug, q_aug, preferred_element_type=F32)

    def later_segments(run):
        x0 = jnp.where(sub < 7, pltpu.roll(run, 7, 0), 1.0)
        s1 = x0 * jnp.where(sub < 7, pltpu.roll(x0, 7, 0), 1.0)
        s2 = s1 * jnp.where(sub < 6, pltpu.roll(s1, 6, 0), 1.0)
        return s2 * jnp.where(sub < 4, pltpu.roll(s2, 4, 0), 1.0)

    def weights(z_ref, a_ref, t_ref, carry, key_off):
        carries = []
        for ci, c0 in enumerate(range(0, w2, SB_LANES)):
            cs = slice(c0, c0 + SB_LANES)
            run2 = [jnp.ones((8, SB_LANES), F32) for _ in range(SB_PARTS)]
            held = [None] * SB_PARTS
            for r in reversed(range(SB_SEG)):
                for h in range(SB_PARTS):
                    row0 = h * part + r * 8
                    th = jnp.tanh(z_ref[row0:row0 + 8, cs])
                    if key_off is not None:
                        th = jnp.where(diags[ci] > key_off + h * part + r, th, -1.0)
                    p = run2[h] * th
                    if r % 2:
                        held[h] = run2[h] + p
                    else:
                        t_ref[row0:row0 + 16, cs] = jnp.concatenate([run2[h] + p, held[h]], axis=0).astype(BF16)
                    run2[h] = run2[h] - p
            cc = carry[ci]
            for h in reversed(range(SB_PARTS)):
                run = run2[h] * (2.0 ** -SB_SEG)
                off = later_segments(run)
                base = off * cc
                base16 = jnp.concatenate([base, base], axis=0).astype(BF16)
                t3 = t_ref[h * part:(h + 1) * part, cs].reshape(part // 16, 16, SB_LANES)
                a_ref[h * part:(h + 1) * part, cs] = (t3 * base16[None]).reshape(part, SB_LANES)
                cc = cc * (off[0:1, :] * run[0:1, :])
            carries.append(cc)
        return tuple(carries)

    def values(kj, a_ref):
        acc_ref[...] += jnp.dot(vT_ref[0, kj], a_ref[...], preferred_element_type=F32)

    def step(b, carry, cur, key_off, first=False):
        if not first:
            values(top - (b - 1), a_refs[1 - cur])
        scores(top - (b + 1), z_refs[1 - cur])
        return weights(z_refs[cur], a_refs[cur], t_refs[cur], carry, key_off)

    scores(top, z_refs[0])
    carry = tuple(jnp.ones((8, SB_LANES), F32) for _ in range(w2 // SB_LANES))
    for b in range(kq):
        carry = step(b, carry, b % 2, SB_BLK * (kq - 1 - b), first=(b == 0))

    def pair(i, carry):
        b = kq + 2 * i
        return step(b + 1, step(b, carry, 0, None), 1, None)

    lax.fori_loop(0, (kq * qi) // 2, pair, carry)
    values(0, a_refs[1])

    oT = jnp.concatenate([acc_ref[:HEAD_DIM, :SB_QBLK], acc_ref[HEAD_DIM:, SB_QBLK:]], axis=0)
    o_ref[...] = (oT.T * zb_ref[...].astype(F32)).astype(BF16)


def _sb_prompt(q_s, k_bf, v_bf, gate_b, sb_bias):
    t = q_s.shape[0]
    nb = t // SB_BLK
    npair = N_HEADS // HEADS_PER_STEP
    w2 = HEADS_PER_STEP * SB_QBLK
    assert (SB_QBLK // SB_BLK) % 2 == 0 and t % SB_QBLK == 0
    k_perm = k_bf.reshape(nb, SB_PARTS, 8, SB_SEG, D_MODEL).transpose(0, 1, 3, 2, 4).reshape(t, D_MODEL)
    vT = v_bf.reshape(nb, SB_PARTS, 8, SB_SEG, npair, PAIR_W).transpose(4, 0, 5, 1, 3, 2)
    vT = vT.reshape(npair, nb, PAIR_W, SB_BLK)
    hbias = jnp.repeat(0.5 * sb_bias.astype(F32), SB_QBLK).reshape(npair, 1, w2)
    hbias = jnp.broadcast_to(hbias, (npair, 8, w2))
    return pl.pallas_call(
        _sb_prompt_kernel,
        grid=(npair, t // SB_QBLK),
        in_specs=[
            pl.BlockSpec((1, 8, w2), lambda p, i: (p, 0, 0)),
            pl.BlockSpec((SB_QBLK, PAIR_W), lambda p, i: (i, p)),
            pl.BlockSpec((t, PAIR_W), lambda p, i: (0, p)),
            pl.BlockSpec((1, nb, PAIR_W, SB_BLK), lambda p, i: (p, 0, 0, 0)),
            pl.BlockSpec((SB_QBLK, PAIR_W), lambda p, i: (i, p)),
        ],
        out_specs=pl.BlockSpec((SB_QBLK, PAIR_W), lambda p, i: (i, p)),
        out_shape=jax.ShapeDtypeStruct((t, D_MODEL), BF16),
        scratch_shapes=([pltpu.VMEM((SB_BLK, w2), F32)] * 2
                        + [pltpu.VMEM((SB_BLK, w2), BF16)] * 2
                        + [pltpu.VMEM((SB_BLK, w2), BF16)] * 2
                        + [pltpu.VMEM((PAIR_W, w2), F32)]),
        compiler_params=_cparams(("arbitrary", "arbitrary")),
        name="sb_prompt",
    )(hbias, q_s, k_perm, vT, gate_b)


SAMPLE_PAGES_PER_STEP = 16
SAMPLE_UNIT_PAGES = 2


def _sb_sample_kernel(pt_ref, qbd_ref, bias_ref, knew_ref, vnew_ref, zb_ref, *refs, n_tok):
    pg = SAMPLE_PAGES_PER_STEP
    k_refs = refs[:pg]
    v_refs = refs[pg:2 * pg]
    o_ref = refs[2 * pg]
    acc_ref, carry_ref = refs[2 * pg + 1:]
    j = pl.program_id(1)
    nrow = n_tok * N_HEADS
    unit = SAMPLE_UNIT_PAGES * PAGE
    qbd = qbd_ref[0]
    bias = bias_ref[...]
    ri = lax.broadcasted_iota(jnp.int32, (unit, unit), 0)
    ci = lax.broadcasted_iota(jnp.int32, (unit, unit), 1)
    later = (ri > ci).astype(BF16)

    def pages(p_refs, first, count):
        return jnp.concatenate([p_refs[first + n][0].astype(BF16) for n in range(count)], axis=1)

    def local(z, keep):
        lse = jnp.log2(1.0 + jnp.exp2(-jnp.abs(z)))
        lb = jnp.minimum(z, 0.0) - lse
        lr = lb - z
        if keep is not None:
            lr = jnp.where(keep, lr, 0.0)
        hi = lr.astype(BF16)
        lo = (lr - hi.astype(F32)).astype(BF16)
        after = (jnp.dot(hi, later, preferred_element_type=F32)
                 + jnp.dot(lo, later, preferred_element_type=F32))
        return lb + after, jnp.sum(lr, axis=1, keepdims=True)

    @pl.when(j == 0)
    def _():
        fill = jnp.zeros((unit - knew_ref.shape[1], D_MODEL), BF16)
        kn = jnp.concatenate([knew_ref[0], fill], axis=0)
        vn = jnp.concatenate([vnew_ref[0], fill], axis=0)
        rr = lax.broadcasted_iota(jnp.int32, (nrow, unit), 0)
        cc = lax.broadcasted_iota(jnp.int32, (nrow, unit), 1)
        keep = jnp.logical_and(cc < rr // N_HEADS, cc < n_tok)
        z = lax.dot_general(qbd, kn, (((1,), (1,)), ((), ())), preferred_element_type=F32) + bias
        pre, tot = local(z, keep)
        a = jnp.where(keep, jnp.exp2(pre), 0.0).astype(BF16)
        acc_ref[...] = jnp.dot(a, vn, preferred_element_type=F32)
        carry_ref[...] = jnp.broadcast_to(tot, carry_ref.shape)

    n_unit = pg // SAMPLE_UNIT_PAGES
    z = jnp.dot(qbd, pages(k_refs, 0, pg), preferred_element_type=F32)
    z = z + jnp.concatenate([bias] * n_unit, axis=1)
    lse = jnp.log2(1.0 + jnp.exp2(-jnp.abs(z)))
    lb = jnp.minimum(z, 0.0) - lse
    lr = lb - z
    cols = [slice(u * unit, (u + 1) * unit) for u in range(n_unit)]
    lr_rows = jnp.concatenate([lr[:, c] for c in cols], axis=0)
    hi = lr_rows.astype(BF16)
    lo = (lr_rows - hi.astype(F32)).astype(BF16)
    both = jnp.dot(jnp.concatenate([hi, lo], axis=0), later, preferred_element_type=F32)
    after = both[:n_unit * nrow] + both[n_unit * nrow:]
    carry = carry_ref[...]
    a_cols = [None] * n_unit
    for u in reversed(range(n_unit)):
        a_cols[u] = jnp.exp2(lb[:, cols[u]] + after[u * nrow:(u + 1) * nrow] + carry)
        carry = carry + jnp.sum(lr[:, cols[u]], axis=1, keepdims=True)
    a = jnp.concatenate(a_cols, axis=1).astype(BF16)
    acc_ref[...] += lax.dot_general(a, pages(v_refs, 0, pg), (((1,), (1,)), ((), ())),
                                    preferred_element_type=F32)
    carry_ref[...] = carry

    @pl.when(j == pl.num_programs(1) - 1)
    def _():
        hcol = lax.broadcasted_iota(jnp.int32, (N_HEADS, D_MODEL), 1) // HEAD_DIM
        hrow = lax.broadcasted_iota(jnp.int32, (N_HEADS, D_MODEL), 0)
        sel = hcol == hrow
        rows = []
        for t in range(n_tok):
            blk = acc_ref[t * N_HEADS:(t + 1) * N_HEADS, :]
            rows.append(jnp.sum(jnp.where(sel, blk, 0.0), axis=0, keepdims=True))
        o = jnp.concatenate(rows, axis=0)
        o_ref[0] = (o * zb_ref[0].astype(F32)).astype(BF16)


def _sb_sample(q_s, k_new, v_new, gate_b, sb_bias, cache_k, cache_v, page_table, n_tok):
    dec_b, n_pages = page_table.shape
    n_pool = cache_k.shape[0]
    pg = SAMPLE_PAGES_PER_STEP
    npg = n_pages // pg
    nrow = n_tok * N_HEADS
    unit = SAMPLE_UNIT_PAGES * PAGE
    ck = cache_k.transpose(0, 2, 3, 1).reshape(n_pool, D_MODEL, PAGE)
    cv = cache_v.transpose(0, 2, 3, 1).reshape(n_pool, D_MODEL, PAGE)
    q4 = (q_s.astype(F32) * (2.0 * LOG2E)).reshape(dec_b, n_tok, N_HEADS, HEAD_DIM)
    eye = jnp.eye(N_HEADS, dtype=F32)
    qbd = jnp.einsum('bthd,hg->bthgd', q4, eye).reshape(dec_b, nrow, D_MODEL).astype(BF16)
    bias = jnp.broadcast_to(jnp.tile(LOG2E * sb_bias.astype(F32), n_tok)[:, None], (nrow, unit))
    new_rows = 16
    pad_rows = ((0, 0), (0, new_rows - n_tok), (0, 0))
    knew = jnp.pad(k_new.reshape(dec_b, n_tok, D_MODEL), pad_rows).astype(BF16)
    vnew = jnp.pad(v_new.reshape(dec_b, n_tok, D_MODEL), pad_rows).astype(BF16)
    gate3 = gate_b.reshape(dec_b, n_tok, D_MODEL)

    def page_spec(i):
        return pl.BlockSpec((1, D_MODEL, PAGE), lambda b, j, pt, i=i: (pt[b, pg * (npg - 1 - j) + i], 0, 0))

    per_seq = lambda shape: pl.BlockSpec(shape, lambda b, j, pt: (b, 0, 0))
    grid_spec = pltpu.PrefetchScalarGridSpec(
        num_scalar_prefetch=1,
        grid=(dec_b, npg),
        in_specs=[per_seq((1, nrow, D_MODEL)),
                  pl.BlockSpec((nrow, unit), lambda b, j, pt: (0, 0)),
                  per_seq((1, new_rows, D_MODEL)), per_seq((1, new_rows, D_MODEL)),
                  per_seq((1, n_tok, D_MODEL))]
                 + [page_spec(i) for i in range(pg)] * 2,
        out_specs=per_seq((1, n_tok, D_MODEL)),
        scratch_shapes=[pltpu.VMEM((nrow, D_MODEL), F32), pltpu.VMEM((nrow, unit), F32)],
    )
    out = pl.pallas_call(
        functools.partial(_sb_sample_kernel, n_tok=n_tok),
        grid_spec=grid_spec,
        out_shape=jax.ShapeDtypeStruct((dec_b, n_tok, D_MODEL), BF16),
        compiler_params=_cparams(("arbitrary", "arbitrary")),
        name="sb_sample",
    )(page_table, qbd, bias, knew, vnew, gate3, *([ck] * pg), *([cv] * pg))
    return out.reshape(dec_b * n_tok, D_MODEL)


def _out_post_kernel(x_ref, ya_ref, yb_ref, p_ref, wo_ref, g_ref, b_ref, wg_ref, wp_ref, o_ref):
    s = (jnp.dot(ya_ref[...], wo_ref[:D_MODEL, :], preferred_element_type=F32)
         + jnp.dot(yb_ref[...], wo_ref[D_MODEL:, :], preferred_element_type=F32))
    o_ref[...] = _post_block(x_ref[...], s, p_ref[...].astype(BF16), g_ref[...], b_ref[...], wg_ref, wp_ref)


def _const_spec(shape):
    nd = len(shape)
    return pl.BlockSpec(shape, lambda i: (0,) * nd, pipeline_mode=pl.Buffered(1))


def _out_post(x, ya, yb, p, wo_bf, g, b, wg_bf, wp_bf, *, tm):
    m = x.shape[0]
    row = lambda w: pl.BlockSpec((tm, w), lambda i: (i, 0))
    return pl.pallas_call(
        _out_post_kernel,
        grid=(m // tm,),
        in_specs=[row(D_MODEL), row(D_MODEL), row(D_MODEL), row(P_DIM),
                  _const_spec(wo_bf.shape), _const_spec((1, D_MODEL)), _const_spec((1, D_MODEL)),
                  _const_spec(wg_bf.shape), _const_spec(wp_bf.shape)],
        out_specs=row(D_MODEL),
        out_shape=jax.ShapeDtypeStruct((m, D_MODEL), F32),
        compiler_params=_cparams(("arbitrary",)),
        name="out_post",
    )(x, ya, yb, p, wo_bf, g.reshape(1, -1), b.reshape(1, -1), wg_bf, wp_bf)


def _odd_kernel(x_ref, p_ref, wu_ref, wv_ref, wz_ref, vg_ref, vb_ref, mix_ref, bmap_ref, wo_ref,
                g_ref, b_ref, wg_ref, wp_ref, *out_refs, tm, emit_v):
    o_ref = out_refs[0]
    x = x_ref[...]
    xb = x.astype(BF16)
    v = jax.nn.gelu(jnp.dot(xb, wv_ref[...], preferred_element_type=F32), approximate=True)
    vn = _layer_norm(v, vg_ref[...], vb_ref[...])
    if emit_v:
        out_refs[1][...] = vn
    vnb = vn.astype(BF16)
    chunks = []
    for c in range(tm // CHUNK):
        cols = []
        for g in range(G_C):
            blk = vnb[c * CHUNK:(c + 1) * CHUNK, g * CHUNK:(g + 1) * CHUNK]
            cols.append(jnp.dot(mix_ref[g], blk, preferred_element_type=F32))
        chunks.append(jnp.concatenate(cols, axis=1) + bmap_ref[...])
    mixed = chunks[0] if len(chunks) == 1 else jnp.concatenate(chunks, axis=0)
    u = jax.nn.gelu(jnp.dot(xb, wu_ref[...], preferred_element_type=F32), approximate=True)
    zz = _silu(jnp.dot(xb, wz_ref[...], preferred_element_type=F32))
    t = (u * mixed * zz).astype(BF16)
    s = jnp.dot(t, wo_ref[...], preferred_element_type=F32)
    o_ref[...] = _post_block(x, s, p_ref[...].astype(BF16), g_ref[...], b_ref[...], wg_ref, wp_ref)


def _odd_layer(x, p, w_in_bf, vg, vb, mix_bf, bmap, wo_bf, g, b, wg_bf, wp_bf, *, tm, emit_v):
    m = x.shape[0]
    row = lambda w: pl.BlockSpec((tm, w), lambda i: (i, 0))
    col = lambda c: pl.BlockSpec((D_MODEL, W_C), lambda i, c=c: (0, c), pipeline_mode=pl.Buffered(1))
    out_shape = [jax.ShapeDtypeStruct((m, D_MODEL), F32)]
    out_specs = [row(D_MODEL)]
    if emit_v:
        out_shape.append(jax.ShapeDtypeStruct((m, W_C), F32))
        out_specs.append(row(W_C))
    return pl.pallas_call(
        functools.partial(_odd_kernel, tm=tm, emit_v=emit_v),
        grid=(m // tm,),
        in_specs=[row(D_MODEL), row(P_DIM), col(0), col(1), col(2),
                  _const_spec((1, W_C)), _const_spec((1, W_C)),
                  _const_spec(mix_bf.shape), _const_spec(bmap.shape), _const_spec(wo_bf.shape),
                  _const_spec((1, D_MODEL)), _const_spec((1, D_MODEL)),
                  _const_spec(wg_bf.shape), _const_spec(wp_bf.shape)],
        out_specs=out_specs,
        out_shape=out_shape,
        compiler_params=_cparams(("arbitrary",)),
        name="odd_layer_sample" if emit_v else "odd_layer_prompt",
    )(x, p, w_in_bf, w_in_bf, w_in_bf, vg.reshape(1, -1), vb.reshape(1, -1), mix_bf, bmap, wo_bf,
      g.reshape(1, -1), b.reshape(1, -1), wg_bf, wp_bf)


def _pick_tile(m, pref):
    t = min(pref, m)
    while m % t:
        t //= 2
    return t


def kernel(x_prompt, x_sample, cache_k, cache_v, state_conv, page_table, p_prompt, p_sample,
           w_in_e, conv_w, w_out_e, sb_bias, w_in_o, vnorm_g, vnorm_b, w_s, b_s, w_out_o,
           ln_g, ln_b, w_ple, w_ple_gate):
    bsz, t_p, _ = x_prompt.shape
    assert bsz == 1
    dec_b, n_tok, _ = x_sample.shape
    m_s = dec_b * n_tok
    assert m_s == CHUNK and t_p % SB_QBLK == 0
    yp = x_prompt.reshape(t_p, D_MODEL)
    ys = x_sample.reshape(m_s, D_MODEL)
    outs = {}

    for i in range(DEPTH):
        li = i // 2
        wg_bf = w_ple_gate[i].astype(BF16)
        wp_bf = w_ple[i].astype(BF16)
        pp = p_prompt[i].reshape(t_p, P_DIM)
        ps = p_sample[i].reshape(m_s, P_DIM)
        if i % 2 == 0:
            w_in_bf = w_in_e[li].astype(BF16)
            wo_bf = w_out_e[li].astype(BF16)
            ya, q_s, k_bf, v_bf, gate_b, k32, v32, u_tail = _even_in(
                yp, w_in_bf, conv_w[li], tm=_pick_tile(t_p, 512))
            yb = _sb_prompt(q_s, k_bf, v_bf, gate_b, sb_bias[li])
            st = state_conv[li]
            zeros = jnp.zeros((dec_b, n_tok - 1, D_MODEL), F32)
            p1 = jnp.concatenate([st[:, 1:2], zeros], axis=1).reshape(m_s, D_MODEL)
            p2 = jnp.concatenate([st, zeros[:, 1:]], axis=1).reshape(m_s, D_MODEL)
            ya_s, q_ss, _, _, gate_s, k32_s, v32_s, u_s = _even_in(
                ys, w_in_bf, conv_w[li], (p1, p2), tm=m_s, seq_len=n_tok)
            yb_s = _sb_sample(q_ss, k32_s, v32_s, gate_s, sb_bias[li], cache_k[li], cache_v[li],
                              page_table, n_tok)
            yp = _out_post(yp, ya, yb, pp, wo_bf, ln_g[i], ln_b[i], wg_bf, wp_bf, tm=_pick_tile(t_p, 512))
            ys = _out_post(ys, ya_s, yb_s, ps, wo_bf, ln_g[i], ln_b[i], wg_bf, wp_bf, tm=m_s)
            outs.setdefault('kp', []).append(k32.reshape(bsz, t_p, N_HEADS, HEAD_DIM))
            outs.setdefault('vp', []).append(v32.reshape(bsz, t_p, N_HEADS, HEAD_DIM))
            outs.setdefault('ks', []).append(k32_s.reshape(dec_b, n_tok, N_HEADS, HEAD_DIM))
            outs.setdefault('vs', []).append(v32_s.reshape(dec_b, n_tok, N_HEADS, HEAD_DIM))
            outs.setdefault('cp', []).append(u_tail[8 - (CONV_W - 1):].reshape(bsz, CONV_W - 1, D_MODEL))
            outs.setdefault('cs', []).append(
                u_s.reshape(dec_b, n_tok, D_MODEL)[:, n_tok - (CONV_W - 1):])
        else:
            w_in_bf = w_in_o[li].astype(BF16)
            wo_bf = w_out_o[li].astype(BF16)
            tril = jnp.tril(w_s[li])
            bmap_p = jnp.repeat(b_s[li].T, W_C // G_C, axis=1)
            small = tril[:, :n_tok, :n_tok]
            mix_s = jnp.einsum('ab,gts->gatbs', jnp.eye(dec_b, dtype=F32), small).reshape(G_C, m_s, m_s)
            bmap_s = jnp.tile(bmap_p[:n_tok], (dec_b, 1))
            (yp,) = _odd_layer(yp, pp, w_in_bf, vnorm_g[li], vnorm_b[li], tril.astype(BF16), bmap_p, wo_bf,
                               ln_g[i], ln_b[i], wg_bf, wp_bf, tm=_pick_tile(t_p, 512), emit_v=False)
            ys, gv = _odd_layer(ys, ps, w_in_bf, vnorm_g[li], vnorm_b[li], mix_s.astype(BF16), bmap_s, wo_bf,
                                ln_g[i], ln_b[i], wg_bf, wp_bf, tm=m_s, emit_v=True)
            outs.setdefault('gv', []).append(gv.reshape(dec_b, n_tok, W_C))

    return (yp.reshape(bsz, t_p, D_MODEL), ys.reshape(dec_b, n_tok, D_MODEL),
            jnp.stack(outs['kp']), jnp.stack(outs['vp']), jnp.stack(outs['ks']), jnp.stack(outs['vs']),
            jnp.stack(outs['cp']), jnp.stack(outs['cs']), jnp.stack(outs['gv']))
```
